```python
import math
import jax, jax.numpy as jnp
from jax import lax
import numpy as np

D_MODEL = 1024
BATCH = 16
SEQ = 2048
DEPTH = 1

EPS = 1e-5
HYENA_WIDTH = D_MODEL // 2
SHORT_CONV = 3
FILTER_EMB_DIM = 17
FILTER_BANDS = (FILTER_EMB_DIM - 1) // 2
FILTER_HIDDEN = 64
DECAY_TARGET = 1e-2
FAST_DECAY_PCT = 0.3
SLOW_DECAY_PCT = 1.5
ATTN_HEADS = 4
ATTN_HEAD_DIM = 64
ATTN_V_DIM = 2 * ATTN_HEAD_DIM
ATTN_WIDTH = ATTN_HEADS * ATTN_V_DIM
Q_BLOCK = 128
N_BRANCHES = 2
IN_COLS = 3 * HYENA_WIDTH + 3 * ATTN_WIDTH + N_BRANCHES * D_MODEL
N_EXPERTS = 32
TOP_K = 4
D_EXPERT = D_MODEL
SWIGLU_LIMIT = 7.0
SWIGLU_ALPHA = 1.702
EXPERT_BLOCK = 256

kernel_name = "hybrid_hyena_diffattn_moe_adaln"


def rms_norm(x, g):
    xf = x.astype(jnp.float32)
    y = xf * lax.rsqrt(jnp.mean(xf * xf, axis=-1, keepdims=True) + EPS)
    return (y * g.astype(jnp.float32)).astype(x.dtype)


def short_conv(u, w, b):
    L = u.shape[1]
    up = jnp.pad(u, ((0, 0), (1, 1), (0, 0)))
    return up[:, :L] * w[0] + up[:, 1:L + 1] * w[1] + up[:, 2:] * w[2] + b


def implicit_filters(L, w1, b1, w2, b2, w3, b3, w4, freq):
    f32 = jnp.float32
    t = jnp.linspace(0.0, 1.0, L, dtype=f32)[:, None]
    w = 2.0 * math.pi * jnp.arange(L, dtype=f32)[:, None] / L
    fr = jnp.linspace(1e-4, FILTER_BANDS - 1, FILTER_BANDS, dtype=f32)[None, :]
    z = jnp.concatenate([t, jnp.cos(fr * w), -jnp.sin(fr * w)], axis=-1)
    fq = freq.astype(f32)
    h = jnp.sin(fq * (z @ w1.astype(f32) + b1.astype(f32)))
    h = jnp.sin(fq * (h @ w2.astype(f32) + b2.astype(f32)))
    h = jnp.sin(fq * (h @ w3.astype(f32) + b3.astype(f32)))
    k = h @ w4.astype(f32)
    min_decay = math.log(DECAY_TARGET) / SLOW_DECAY_PCT
    max_decay = math.log(DECAY_TARGET) / FAST_DECAY_PCT
    deltas = jnp.linspace(min_decay, max_decay, HYENA_WIDTH, dtype=f32)
    decay = jnp.exp(-t * jnp.abs(deltas))
    k = k.reshape(L, 2, HYENA_WIDTH) * decay[:, None, :]
    return k[:, 0], k[:, 1]


def two_sided_long_conv(v, k_fwd, k_bwd):
    B, L, C = v.shape
    k_full = jnp.concatenate([k_fwd, jnp.zeros((1, C), jnp.float32), k_bwd[1:][::-1]], axis=0)
    kf = jnp.fft.rfft(k_full, n=2 * L, axis=0)
    vf = jnp.fft.rfft(v.astype(jnp.float32), n=2 * L, axis=1)
    y = jnp.fft.irfft(vf * kf[None], n=2 * L, axis=1)[:, :L]
    return y.astype(v.dtype)


def hyena_branch(u3, conv_w, conv_b, k_fwd, k_bwd, d_skip):
    uc = short_conv(u3, conv_w, conv_b)
    x0, x1, v = jnp.split(uc, 3, axis=-1)
    v = v * x1
    y = two_sided_long_conv(v, k_fwd, k_bwd) + v * d_skip
    return y * x0


def diff_attention(q, k, v, lam, lam_init, subln_g):
    B, L = q.shape[0], q.shape[1]
    nb = L // Q_BLOCK
    scale = 1.0 / math.sqrt(ATTN_HEAD_DIM)
    slopes = 2.0 ** (-8.0 * jnp.arange(1, ATTN_HEADS + 1, dtype=jnp.float32) / ATTN_HEADS)
    pos = jnp.arange(L, dtype=jnp.float32)
    kf = k.astype(jnp.float32)
    vf = v.astype(jnp.float32)
    qb = q.astype(jnp.float32).reshape(B, nb, Q_BLOCK, ATTN_HEADS, 2, ATTN_HEAD_DIM).transpose(1, 0, 2, 3, 4, 5)
    qpos = pos.reshape(nb, Q_BLOCK)

    def block(args):
        q_blk, q_pos = args
        s = jnp.einsum('bqhcd,bkhcd->bhcqk', q_blk, kf) * scale
        dist = jnp.abs(q_pos[:, None] - pos[None, :])
        s = s - slopes[None, :, None, None, None] * dist[None, None, None]
        p = jax.nn.softmax(s, axis=-1)
        a = p[:, :, 0] - lam * p[:, :, 1]
        return jnp.einsum('bhqk,bkhe->bqhe', a, vf)

    out = lax.map(block, (qb, qpos))
    out = out.transpose(1, 0, 2, 3, 4).reshape(B, L, ATTN_HEADS, ATTN_V_DIM)
    out = rms_norm(out, subln_g) * (1.0 - lam_init)
    return out.reshape(B, L, ATTN_WIDTH).astype(q.dtype)


def moe_ffn(h, w_router, b_router, w_gate_up, b_gate_up, w_down, b_down):
    B, L, D = h.shape
    T = B * L
    hf = h.reshape(T, D)
    logits = (hf @ w_router + b_router).astype(jnp.float32)
    top_val, top_idx = lax.top_k(logits, TOP_K)
    gates = jax.nn.softmax(top_val, axis=-1)
    P = T * TOP_K
    flat_e = top_idx.reshape(P)
    order = jnp.argsort(flat_e)
    e_sorted = flat_e[order]
    tok_sorted = (order // TOP_K).astype(jnp.int32)
    w_sorted = gates.reshape(P)[order]
    counts = jnp.zeros((N_EXPERTS,), jnp.int32).at[flat_e].add(1)
    padded = ((counts + EXPERT_BLOCK - 1) // EXPERT_BLOCK) * EXPERT_BLOCK
    pad_end = jnp.cumsum(padded)
    pad_start = pad_end - padded
    raw_start = jnp.cumsum(counts) - counts
    rank = jnp.arange(P, dtype=jnp.int32) - raw_start[e_sorted]
    dest = pad_start[e_sorted] + rank
    n_blocks = -(-P // EXPERT_BLOCK) + N_EXPERTS
    n_rows = n_blocks * EXPERT_BLOCK
    row_tok = jnp.zeros((n_rows,), jnp.int32).at[dest].set(tok_sorted)
    row_w = jnp.zeros((n_rows,), jnp.float32).at[dest].set(w_sorted)
    block_start = jnp.arange(n_blocks, dtype=jnp.int32) * EXPERT_BLOCK
    block_expert = jnp.minimum(jnp.searchsorted(pad_end, block_start, side='right'), N_EXPERTS - 1)

    def expert_block(args):
        toks, wts, e = args
        xb = hf[toks]
        gu = xb @ w_gate_up[e] + b_gate_up[e]
        gate, up = gu[:, :D_EXPERT], gu[:, D_EXPERT:]
        gate = jnp.minimum(gate, SWIGLU_LIMIT)
        up = jnp.clip(up, -SWIGLU_LIMIT, SWIGLU_LIMIT)
        glu = gate * jax.nn.sigmoid(SWIGLU_ALPHA * gate)
        y = ((up + 1.0) * glu) @ w_down[e] + b_down[e]
        return y * wts[:, None].astype(y.dtype)

    ys = lax.map(expert_block, (row_tok.reshape(n_blocks, EXPERT_BLOCK),
                                row_w.reshape(n_blocks, EXPERT_BLOCK), block_expert))
    out = jnp.zeros((T, D), h.dtype).at[row_tok].add(ys.reshape(n_rows, D).astype(h.dtype))
    return out.reshape(B, L, D)


def setup_inputs(seed: int = 0) -> dict:
    key = jax.random.key(seed)
    ks = jax.random.split(key, 32)
    f32 = jnp.float32
    D, HW, E, DE, FH = D_MODEL, HYENA_WIDTH, N_EXPERTS, D_EXPERT, FILTER_HIDDEN
    n = lambda k, shape, s: (jax.random.normal(k, shape, f32) * s)
    return {
        "x": n(ks[0], (BATCH, SEQ, D), 1.0),
        "c": n(ks[1], (BATCH, D), 1.0),
        "w_ada": n(ks[2], (DEPTH, D, 6 * D), 0.5 * D ** -0.5),
        "b_ada": n(ks[3], (DEPTH, 6 * D), 0.02),
        "norm_mix_g": 1.0 + n(ks[4], (DEPTH, D), 0.02),
        "w_in": n(ks[5], (DEPTH, D, IN_COLS), D ** -0.5),
        "b_in": n(ks[6], (DEPTH, IN_COLS), 0.02),
        "hy_conv_w": n(ks[7], (DEPTH, SHORT_CONV, 3 * HW), SHORT_CONV ** -0.5),
        "hy_conv_b": n(ks[8], (DEPTH, 3 * HW), 0.02),
        "filt_w1": n(ks[9], (DEPTH, FILTER_EMB_DIM, FH), FILTER_EMB_DIM ** -0.5),
        "filt_b1": n(ks[10], (DEPTH, FH), 0.02),
        "filt_w2": n(ks[11], (DEPTH, FH, FH), FH ** -0.5),
        "filt_b2": n(ks[12], (DEPTH, FH), 0.02),
        "filt_w3": n(ks[13], (DEPTH, FH, FH), FH ** -0.5),
        "filt_b3": n(ks[14], (DEPTH, FH), 0.02),
        "filt_w4": n(ks[15], (DEPTH, FH, 2 * HW), 0.2 * FH ** -0.5),
        "filt_freq": 1.0 + n(ks[16], (DEPTH, FH), 0.02),
        "hy_d_skip": n(ks[17], (DEPTH, HW), 1.0),
        "lambda_qk": n(ks[18], (DEPTH, 4, ATTN_HEAD_DIM), 0.1),
        "attn_subln_g": 1.0 + n(ks[19], (DEPTH, ATTN_V_DIM), 0.02),
        "w_hy_out": n(ks[20], (DEPTH, HW, D), HW ** -0.5),
        "w_attn_out": n(ks[21], (DEPTH, ATTN_WIDTH, D), ATTN_WIDTH ** -0.5),
        "w_mix_out": n(ks[22], (DEPTH, D, D), D ** -0.5),
        "norm_ffn_g": 1.0 + n(ks[23], (DEPTH, D), 0.02),
        "w_router": n(ks[24], (DEPTH, D, E), D ** -0.5),
        "b_router": n(ks[25], (DEPTH, E), 0.01),
        "w_gate_up": n(ks[26], (DEPTH, E, D, 2 * DE), D ** -0.5),
        "b_gate_up": n(ks[27], (DEPTH, E, 2 * DE), 0.01),
        "w_down": n(ks[28], (DEPTH, E, DE, D), DE ** -0.5),
        "b_down": n(ks[29], (DEPTH, E, D), 0.01),
        "final_norm_g": 1.0 + n(ks[30], (D,), 0.02),
    }


def reference(x, c, w_ada, b_ada, norm_mix_g, w_in, b_in, hy_conv_w, hy_conv_b,
              filt_w1, filt_b1, filt_w2, filt_b2, filt_w3, filt_b3, filt_w4, filt_freq,
              hy_d_skip, lambda_qk, attn_subln_g, w_hy_out, w_attn_out, w_mix_out,
              norm_ffn_g, w_router, b_router, w_gate_up, b_gate_up, w_down, b_down,
              final_norm_g):
    B, L, D = x.shape
    split_idx = [3 * HYENA_WIDTH, 3 * HYENA_WIDTH + ATTN_WIDTH,
                 3 * HYENA_WIDTH + 2 * ATTN_WIDTH, 3 * HYENA_WIDTH + 3 * ATTN_WIDTH]
    for l in range(DEPTH):
        lam_init = 0.8 - 0.6 * math.exp(-0.3 * l)
        mod = jax.nn.silu(c) @ w_ada[l] + b_ada[l]
        shift1, scale1, gate1, shift2, scale2, gate2 = jnp.split(mod[:, None, :], 6, axis=-1)

        h = rms_norm(x, norm_mix_g[l]) * (1.0 + scale1) + shift1
        u = h @ w_in[l] + b_in[l]
        u_hy, q, k, v, g = jnp.split(u, split_idx, axis=-1)

        k_fwd, k_bwd = implicit_filters(L, filt_w1[l], filt_b1[l], filt_w2[l], filt_b2[l],
                                        filt_w3[l], filt_b3[l], filt_w4[l], filt_freq[l])
        y_hy = hyena_branch(u_hy, hy_conv_w[l], hy_conv_b[l], k_fwd, k_bwd, hy_d_skip[l])

        lq = lambda_qk[l].astype(jnp.float32)
        lam = jnp.exp(jnp.sum(lq[0] * lq[1])) - jnp.exp(jnp.sum(lq[2] * lq[3])) + lam_init
        q = q.reshape(B, L, ATTN_HEADS, 2, ATTN_HEAD_DIM)
        k = k.reshape(B, L, ATTN_HEADS, 2, ATTN_HEAD_DIM)
        v = v.reshape(B, L, ATTN_HEADS, ATTN_V_DIM)
        y_at = diff_attention(q, k, v, lam, lam_init, attn_subln_g[l])

        gts = jax.nn.sigmoid(g.reshape(B, L, N_BRANCHES, D))
        merged = gts[:, :, 0] * (y_hy @ w_hy_out[l]) + gts[:, :, 1] * (y_at @ w_attn_out[l])
        x = x + gate1 * (merged @ w_mix_out[l])

        h = rms_norm(x, norm_ffn_g[l]) * (1.0 + scale2) + shift2
        x = x + gate2 * moe_ffn(h, w_router[l], b_router[l], w_gate_up[l], b_gate_up[l],
                                w_down[l], b_down[l])
    return rms_norm(x, final_norm_g)
```

```python
import functools
import math

import numpy as np
import jax
import jax.numpy as jnp
from jax import lax
from jax.experimental import pallas as pl
from jax.experimental.pallas import tpu as pltpu

F32 = jnp.float32
BF16 = jnp.bfloat16
I32 = jnp.int32

EPS = 1e-5
HYENA_WIDTH = 512
FILTER_BANDS = 8
DECAY_TARGET = 1e-2
FAST_DECAY_PCT = 0.3
SLOW_DECAY_PCT = 1.5
ATTN_HEADS = 4
ATTN_HEAD_DIM = 64
ATTN_V_DIM = 2 * ATTN_HEAD_DIM
ATTN_WIDTH = ATTN_HEADS * ATTN_V_DIM
N_EXPERTS = 32
TOP_K = 4
SWIGLU_LIMIT = 7.0
SWIGLU_ALPHA = 1.702
LAMBDA_INIT = 0.8 - 0.6 * math.exp(-0.3 * 0)

IN_PROJ_ROWS = 512
ATTN_Q_ROWS = 512
MERGE_ROWS = 512
ROUTE_TOKENS = 512
DEST_TOKENS = 2048
DISPATCH_TOKENS = 256
EXPERT_ROWS = 512
COMBINE_TOKENS = 256
ROUTE_SUBLANES = 8

VMEM_LIMIT_BYTES = 56 * 1024 * 1024


def _tile(n, t):
    t = min(n, t)
    assert n % t == 0, (n, t)
    return t


def _params(n_axes):
    return pltpu.CompilerParams(
        dimension_semantics=("arbitrary",) * n_axes, vmem_limit_bytes=VMEM_LIMIT_BYTES
    )


def _split_bf16(a):
    hi = a.astype(BF16)
    lo = (a - hi.astype(F32)).astype(BF16)
    return hi, lo


_NN = (((1,), (0,)), ((), ()))
_NT = (((1,), (1,)), ((), ()))


def _dot3(a, b, dims=_NN):
    ah, al = _split_bf16(a)
    bh, bl = _split_bf16(b)
    d = lambda x, y: lax.dot_general(x, y, dims, preferred_element_type=F32)
    return d(ah, bh) + d(ah, bl) + d(al, bh)


def _dot(a, b):
    return jnp.dot(a, b, preferred_element_type=F32)


def _rms(x):
    return x * lax.rsqrt(jnp.mean(x * x, axis=-1, keepdims=True) + EPS)


def _mod_kernel(c_ref, w_ref, b_ref, o_ref):
    c = c_ref[...]
    o_ref[...] = _dot3(c * jax.nn.sigmoid(c), w_ref[...]) + b_ref[...]


def _adaln_mod(c, w, b):
    bsz, d = c.shape
    n = w.shape[1]
    tn = _tile(n, 1536)
    return pl.pallas_call(
        _mod_kernel,
        grid=(n // tn,),
        in_specs=[
            pl.BlockSpec((bsz, d), lambda j: (0, 0)),
            pl.BlockSpec((d, tn), lambda j: (0, j)),
            pl.BlockSpec((1, tn), lambda j: (0, j)),
        ],
        out_specs=pl.BlockSpec((bsz, tn), lambda j: (0, j)),
        out_shape=jax.ShapeDtypeStruct((bsz, n), F32),
        compiler_params=_params(1),
        name="adaln_mod",
    )(c, w, b)


def _in_proj_kernel(x_ref, mod_ref, g_ref, w_ref, b_ref, uhy_ref, q_ref, k_ref, v_ref, sg_ref, *, cols):
    x = x_ref[0]
    h = _rms(x) * g_ref[...]
    h = h * (1.0 + mod_ref[0, 1:2, :]) + mod_ref[0, 0:1, :]
    hb = h.astype(BF16)

    def proj(lo, hi):
        return _dot(hb, w_ref[:, lo:hi]) + b_ref[:, lo:hi]

    c0, c1, c2, c3, c4 = cols
    uhy_ref[0] = proj(0, c0).astype(BF16)
    q_ref[0] = (proj(c0, c1) * (1.0 / math.sqrt(ATTN_HEAD_DIM))).astype(BF16)
    k_ref[0] = proj(c1, c2).astype(BF16)
    v_ref[0] = proj(c2, c3).astype(BF16)
    sg_ref[0] = jax.nn.sigmoid(proj(c3, c4)).astype(BF16)


def _in_proj(x, mod3, g, w_bf16, b):
    bsz, seq, d = x.shape
    n = w_bf16.shape[1]
    hw3 = 3 * HYENA_WIDTH
    cols = (hw3, hw3 + ATTN_WIDTH, hw3 + 2 * ATTN_WIDTH, hw3 + 3 * ATTN_WIDTH, n)
    tm = _tile(seq, IN_PROJ_ROWS)
    row = lambda width: pl.BlockSpec((1, tm, width), lambda bi, i: (bi, i, 0))
    out = lambda width: jax.ShapeDtypeStruct((bsz, seq, width), BF16)
    return pl.pallas_call(
        functools.partial(_in_proj_kernel, cols=cols),
        grid=(bsz, seq // tm),
        in_specs=[
            row(d),
            pl.BlockSpec((1, 6, d), lambda bi, i: (bi, 0, 0)),
            pl.BlockSpec((1, d), lambda bi, i: (0, 0)),
            pl.BlockSpec((d, n), lambda bi, i: (0, 0)),
            pl.BlockSpec((1, n), lambda bi, i: (0, 0)),
        ],
        out_specs=[row(hw3), row(ATTN_WIDTH), row(ATTN_WIDTH), row(ATTN_WIDTH), row(2 * d)],
        out_shape=[out(hw3), out(ATTN_WIDTH), out(ATTN_WIDTH), out(ATTN_WIDTH), out(2 * d)],
        compiler_params=_params(2),
        name="in_proj",
    )(x, mod3, g, w_bf16, b)


@functools.lru_cache(maxsize=None)
def _dft_tables(seq):
    n_fft = 2 * seq
    idx = np.arange(seq, dtype=np.int64)
    ang = 2.0 * np.pi * ((idx[:, None] * idx[None, :]) % n_fft).astype(np.float64) / n_fft
    return np.cos(ang).astype(np.float32), np.sin(ang).astype(np.float32)


def _filter_features(seq):
    t = jnp.linspace(0.0, 1.0, seq, dtype=F32)[:, None]
    w = 2.0 * math.pi * jnp.arange(seq, dtype=F32)[:, None] / seq
    fr = jnp.linspace(1e-4, FILTER_BANDS - 1, FILTER_BANDS, dtype=F32)[None, :]
    z = jnp.concatenate([t, jnp.cos(fr * w), -jnp.sin(fr * w)], axis=-1)
    return jnp.pad(z, ((0, 0), (0, 128 - z.shape[1])))


def _alternating(rows):
    return jnp.where((rows & 1) == 0, 1.0, -1.0).astype(F32)


def _filter_kernel(z_ref, w1_ref, b1_ref, w2_ref, b2_ref, w3_ref, b3_ref, w4_ref, fq_ref, ad_ref,
                   cos_ref, sin_ref, kr_ref, kq_ref):
    seq = z_ref.shape[0]
    width = ad_ref.shape[1]
    fq = fq_ref[...]
    z = z_ref[...]
    h = jnp.sin(fq * (_dot3(z, w1_ref[...]) + b1_ref[...]))
    h = jnp.sin(fq * (_dot3(h, w2_ref[...]) + b2_ref[...]))
    h = jnp.sin(fq * (_dot3(h, w3_ref[...]) + b3_ref[...]))
    k = _dot3(h, w4_ref[...])
    decay = jnp.exp(-z[:, 0:1] * ad_ref[...])
    rows = lax.broadcasted_iota(I32, (seq, 1), 0)
    first = rows == 0
    k_fwd = k[:, :width] * decay
    k_bwd = jnp.where(first, 0.0, k[:, width:] * decay)
    k_sum = k_fwd + k_bwd
    k_dif = k_bwd - k_fwd
    kr = _dot(cos_ref[...], k_sum.astype(BF16))
    kq = _dot(sin_ref[...], k_dif.astype(BF16))
    nyquist = jnp.sum(_alternating(rows) * k_sum, axis=0, keepdims=True)
    kq = jnp.where(first, nyquist, kq)
    scale = jnp.where(first, 0.5 / seq, 1.0 / seq)
    kr_ref[...] = kr * scale
    kq_ref[...] = kq * scale


def _hyena_filters(z, w1, b1, w2, b2, w3, b3, w4, fq, absdelta, cos_b, sin_b):
    seq = z.shape[0]
    width = absdelta.shape[1]
    args = (z, w1, b1, w2, b2, w3, b3, w4, fq, absdelta, cos_b, sin_b)
    full = lambda a: pl.BlockSpec(a.shape, lambda i: (0,) * a.ndim)
    return pl.pallas_call(
        _filter_kernel,
        grid=(1,),
        in_specs=[full(a) for a in args],
        out_specs=[pl.BlockSpec((seq, width), lambda i: (0, 0))] * 2,
        out_shape=[jax.ShapeDtypeStruct((seq, width), F32)] * 2,
        compiler_params=_params(1),
        name="hyena_filters",
    )(*args)


def _hyena_kernel(x0_ref, x1_ref, v_ref, w0_ref, w1_ref, wv_ref, b0_ref, b1_ref, bv_ref, dskip_ref,
                  cos_ref, sin_ref, kr_ref, kq_ref, o_ref):
    seq = x0_ref.shape[1]
    rows = lax.broadcasted_iota(I32, (seq, 1), 0)
    first = rows == 0
    last = rows == seq - 1
    alt = _alternating(rows)

    def short_conv(u_ref, w_ref, b_ref):
        u = u_ref[0].astype(F32)
        prev = jnp.where(first, 0.0, pltpu.roll(u, 1, 0))
        nxt = jnp.where(last, 0.0, pltpu.roll(u, seq - 1, 0))
        w = w_ref[...]
        return prev * w[0:1] + u * w[1:2] + nxt * w[2:3] + b_ref[...]

    x0 = short_conv(x0_ref, w0_ref, b0_ref)
    x1 = short_conv(x1_ref, w1_ref, b1_ref)
    v = short_conv(v_ref, wv_ref, bv_ref) * x1
    vb = v.astype(BF16)
    cos_m = cos_ref[...]
    sin_m = sin_ref[...]
    vr = _dot(cos_m, vb)
    vq = jnp.where(first, jnp.sum(alt * v, axis=0, keepdims=True), _dot(sin_m, vb))
    kr = kr_ref[...]
    kq = kq_ref[...]
    yr = jnp.where(first, vr * kr, vr * kr + vq * kq)
    yq = jnp.where(first, vq * kq, vq * kr - vr * kq)
    y = _dot(cos_m, yr.astype(BF16)) + _dot(sin_m, yq.astype(BF16)) + alt * yq[0:1, :]
    o_ref[0] = ((y + v * dskip_ref[...]) * x0).astype(BF16)


def _hyena(u_hy, conv_w, conv_b, d_skip, cos_b, sin_b, kr, kq):
    bsz, seq, _ = u_hy.shape
    width = d_skip.shape[1]
    cw = _tile(width, 256)
    nc = width // cw
    u_spec = lambda part: pl.BlockSpec((1, seq, cw), lambda c, b: (b, 0, part * nc + c))
    w_spec = lambda part: pl.BlockSpec((3, cw), lambda c, b: (0, part * nc + c))
    b_spec = lambda part: pl.BlockSpec((1, cw), lambda c, b: (0, part * nc + c))
    full = pl.BlockSpec((seq, seq), lambda c, b: (0, 0))
    k_spec = pl.BlockSpec((seq, cw), lambda c, b: (0, c))
    return pl.pallas_call(
        _hyena_kernel,
        grid=(nc, bsz),
        in_specs=[u_spec(0), u_spec(1), u_spec(2), w_spec(0), w_spec(1), w_spec(2),
                  b_spec(0), b_spec(1), b_spec(2), pl.BlockSpec((1, cw), lambda c, b: (0, c)),
                  full, full, k_spec, k_spec],
        out_specs=pl.BlockSpec((1, seq, cw), lambda c, b: (b, 0, c)),
        out_shape=jax.ShapeDtypeStruct((bsz, seq, width), BF16),
        compiler_params=_params(2),
        name="hyena",
    )(u_hy, u_hy, u_hy, conv_w, conv_w, conv_w, conv_b, conv_b, conv_b, d_skip, cos_b, sin_b, kr, kq)


def _attn_kernel(q_ref, k_ref, v_ref, lq_ref, g_ref, o_ref, bias_ref):
    head = pl.program_id(0)
    qi = pl.program_id(1)
    tq = q_ref.shape[1]
    seq = k_ref.shape[1]

    @pl.when(pl.program_id(2) == 0)
    def _():
        slope = jnp.where(head == 0, 0.25, jnp.where(head == 1, 0.0625, jnp.where(head == 2, 0.015625, 0.00390625)))
        i = lax.broadcasted_iota(I32, (tq, seq), 0) + qi * tq
        j = lax.broadcasted_iota(I32, (tq, seq), 1)
        bias_ref[...] = jnp.abs(i - j).astype(F32) * (-slope)

    q = q_ref[0]
    k = k_ref[0]
    v = v_ref[0]
    lane = lax.broadcasted_iota(I32, (1, ATTN_V_DIM), 1)
    zero = jnp.zeros_like(q)

    def softmax_av(qm):
        s = lax.dot_general(qm, k, _NT, preferred_element_type=F32) + bias_ref[...]
        p = jnp.exp(s - jnp.max(s, axis=-1, keepdims=True))
        denom = jnp.sum(p, axis=-1, keepdims=True)
        return _dot(p.astype(BF16), v) / denom

    o1 = softmax_av(jnp.where(lane < ATTN_HEAD_DIM, q, zero))
    o2 = softmax_av(jnp.where(lane >= ATTN_HEAD_DIM, q, zero))
    lq = lq_ref[...]
    lam = (jnp.exp(jnp.sum(lq[0:1] * lq[1:2], axis=-1, keepdims=True))
           - jnp.exp(jnp.sum(lq[2:3] * lq[3:4], axis=-1, keepdims=True)) + LAMBDA_INIT)
    o = o1 - lam * o2
    o_ref[0] = (_rms(o) * g_ref[...] * (1.0 - LAMBDA_INIT)).astype(BF16)


def _diff_attention(q, k, v, lambda_qk, subln_g):
    bsz, seq, _ = q.shape
    tq = _tile(seq, ATTN_Q_ROWS)
    kv_spec = pl.BlockSpec((1, seq, ATTN_V_DIM), lambda h, i, b: (b, 0, h))
    q_spec = pl.BlockSpec((1, tq, ATTN_V_DIM), lambda h, i, b: (b, i, h))
    return pl.pallas_call(
        _attn_kernel,
        grid=(ATTN_HEADS, seq // tq, bsz),
        in_specs=[q_spec, kv_spec, kv_spec,
                  pl.BlockSpec(lambda_qk.shape, lambda h, i, b: (0, 0)),
                  pl.BlockSpec(subln_g.shape, lambda h, i, b: (0, 0))],
        out_specs=q_spec,
        out_shape=jax.ShapeDtypeStruct((bsz, seq, ATTN_WIDTH), BF16),
        scratch_shapes=[pltpu.VMEM((tq, seq), F32)],
        compiler_params=_params(3),
        name="diff_attention",
    )(q, k, v, lambda_qk, subln_g)


def _merge_kernel(yhy_ref, yat_ref, sg_ref, x_ref, mod_ref, why_ref, wat_ref, wmix_ref, g_ref,
                  wr_ref, br_ref, x1_ref, h2_ref, lg_ref):
    d = x_ref.shape[2]
    sg = sg_ref[0]
    merged = (sg[:, :d].astype(F32) * _dot(yhy_ref[0], why_ref[...])
              + sg[:, d:].astype(F32) * _dot(yat_ref[0], wat_ref[...]))
    x1 = x_ref[0] + mod_ref[0, 2:3, :] * _dot(merged.astype(BF16), wmix_ref[...])
    x1_ref[0] = x1
    h2 = _rms(x1) * g_ref[...]
    h2 = h2 * (1.0 + mod_ref[0, 4:5, :]) + mod_ref[0, 3:4, :]
    h2_ref[0] = h2
    lg_ref[0] = _dot3(wr_ref[...], h2, _NT) + br_ref[...]


def _merge(y_hy, y_at, sg, x, mod3, w_hy, w_at, w_mix, g, w_router_t, b_router):
    bsz, seq, d = x.shape
    tm = _tile(seq, MERGE_ROWS)
    row = lambda width: pl.BlockSpec((1, tm, width), lambda b, i: (b, i, 0))
    full = lambda a: pl.BlockSpec(a.shape, lambda b, i: (0,) * a.ndim)
    return pl.pallas_call(
        _merge_kernel,
        grid=(bsz, seq // tm),
        in_specs=[row(y_hy.shape[2]), row(y_at.shape[2]), row(2 * d), row(d),
                  pl.BlockSpec((1, 6, d), lambda b, i: (b, 0, 0)),
                  full(w_hy), full(w_at), full(w_mix), full(g), full(w_router_t), full(b_router)],
        out_specs=[row(d), row(d), pl.BlockSpec((1, N_EXPERTS, tm), lambda b, i: (b, 0, i))],
        out_shape=[jax.ShapeDtypeStruct((bsz, seq, d), F32), jax.ShapeDtypeStruct((bsz, seq, d), F32),
                   jax.ShapeDtypeStruct((bsz, N_EXPERTS, seq), F32)],
        compiler_params=_params(2),
        name="merge_router",
    )(y_hy, y_at, sg, x, mod3, w_hy, w_at, w_mix, g, w_router_t, b_router)


def _route_kernel(lg_ref, idx_ref, gate_ref, rank_ref, cnt_ref, carry_ref):
    step = pl.program_id(0) * pl.num_programs(1) + pl.program_id(1)

    @pl.when(step == 0)
    def _():
        carry_ref[...] = jnp.zeros_like(carry_ref)

    work = lg_ref[0]
    n_exp, tr = work.shape
    e_iota = lax.broadcasted_iota(I32, (n_exp, tr), 0)
    vals, idxs = [], []
    for _ in range(TOP_K):
        m = jnp.max(work, axis=0, keepdims=True)
        ik = jnp.min(jnp.where(work == m, e_iota, n_exp), axis=0, keepdims=True)
        vals.append(m)
        idxs.append(ik)
        work = jnp.where(e_iota == ik, -jnp.inf, work)
    ex = [jnp.exp(val - vals[0]) for val in vals]
    denom = ex[0] + ex[1] + ex[2] + ex[3]
    sel = jnp.zeros((n_exp, tr), F32)
    for ik in idxs:
        sel = sel + jnp.where(e_iota == ik, 1.0, 0.0)
    upper = jnp.where(lax.broadcasted_iota(I32, (tr, tr), 0) < lax.broadcasted_iota(I32, (tr, tr), 1), 1.0, 0.0)
    rank_all = _dot(sel.astype(BF16), upper.astype(BF16)) + carry_ref[:, 0:1]
    pad = ROUTE_SUBLANES - TOP_K
    ranks = [jnp.sum(jnp.where(e_iota == ik, rank_all, 0.0), axis=0, keepdims=True) for ik in idxs]
    idx_ref[...] = jnp.concatenate(idxs + [jnp.zeros((pad, tr), I32)], axis=0)
    gate_ref[...] = jnp.concatenate([e / denom for e in ex] + [jnp.zeros((pad, tr), F32)], axis=0)
    rank_ref[...] = jnp.concatenate(ranks + [jnp.zeros((pad, tr), F32)], axis=0).astype(I32)
    carry_ref[...] = carry_ref[...] + jnp.sum(sel, axis=1, keepdims=True)
    cnt_ref[...] = carry_ref[...]


def _route(logits_t):
    bsz, n_exp, seq = logits_t.shape
    tr = _tile(seq, ROUTE_TOKENS)
    nt = seq // tr
    tok = pl.BlockSpec((ROUTE_SUBLANES, tr), lambda b, i: (0, b * nt + i))
    shape = lambda dt: jax.ShapeDtypeStruct((ROUTE_SUBLANES, bsz * seq), dt)
    return pl.pallas_call(
        _route_kernel,
        grid=(bsz, nt),
        in_specs=[pl.BlockSpec((1, n_exp, tr), lambda b, i: (b, 0, i))],
        out_specs=[tok, tok, tok, pl.BlockSpec((n_exp, 128), lambda b, i: (0, 0))],
        out_shape=[shape(I32), shape(F32), shape(I32), jax.ShapeDtypeStruct((n_exp, 128), F32)],
        scratch_shapes=[pltpu.VMEM((n_exp, 128), F32)],
        compiler_params=_params(2),
        name="route_topk",
    )(logits_t)


def _dest_kernel(idx_ref, rank_ref, start_ref, dest_ref):
    idx = idx_ref[...]
    n_exp = start_ref.shape[0]
    tr = idx.shape[1]
    e_iota = lax.broadcasted_iota(I32, (n_exp, tr), 0)
    start = start_ref[:, 0:1]
    rows = [jnp.sum(jnp.where(e_iota == idx[k:k + 1, :], start, 0), axis=0, keepdims=True) for k in range(TOP_K)]
    rows.append(jnp.zeros((ROUTE_SUBLANES - TOP_K, tr), I32))
    dest_ref[...] = rank_ref[...] + jnp.concatenate(rows, axis=0)


def _dest_rows(idx_t, rank_t, pad_start):
    n_tok = idx_t.shape[1]
    tr = _tile(n_tok, DEST_TOKENS)
    tok = pl.BlockSpec((ROUTE_SUBLANES, tr), lambda i: (0, i))
    return pl.pallas_call(
        _dest_kernel,
        grid=(n_tok // tr,),
        in_specs=[tok, tok, pl.BlockSpec(pad_start.shape, lambda i: (0, 0))],
        out_specs=tok,
        out_shape=jax.ShapeDtypeStruct(idx_t.shape, I32),
        compiler_params=_params(1),
        name="dest_rows",
    )(idx_t, rank_t, pad_start)


def _dispatch_kernel(dest_ref, pend_ref, h_ref, xs_ref, zeros_ref, sem):
    tt = h_ref.shape[0]
    bm = zeros_ref.shape[0]

    @pl.when(pl.program_id(0) == 0)
    def _():
        zeros_ref[...] = jnp.zeros_like(zeros_ref)

        def zero_copy(e):
            start = pl.multiple_of(jnp.maximum(pend_ref[e] - bm, 0), bm)
            return pltpu.make_async_copy(zeros_ref, xs_ref.at[pl.ds(start, bm)], sem)

        for e in range(N_EXPERTS):
            zero_copy(e).start()
        for e in range(N_EXPERTS):
            zero_copy(e).wait()

        def zero_tail(j, carry):
            tail = pltpu.make_async_copy(zeros_ref, xs_ref.at[pl.ds(pl.multiple_of(j * bm, bm), bm)], sem)
            tail.start()
            tail.wait()
            return carry

        lax.fori_loop(pend_ref[N_EXPERTS - 1] // bm, xs_ref.shape[0] // bm, zero_tail, 0)

    def row_copy(t, k):
        return pltpu.make_async_copy(h_ref.at[pl.ds(t, 1)], xs_ref.at[pl.ds(dest_ref[k, t], 1)], sem)

    def issue(t, carry):
        for k in range(TOP_K):
            row_copy(t, k).start()
        return carry

    lax.fori_loop(0, tt, issue, 0)

    def drain(t, carry):
        for k in range(TOP_K):
            row_copy(t, k).wait()
        return carry

    lax.fori_loop(0, tt, drain, 0)


def _dispatch(dest_t, pad_end, h2, n_rows):
    n_tok, d = h2.shape
    tt = _tile(n_tok, DISPATCH_TOKENS)
    return pl.pallas_call(
        _dispatch_kernel,
        grid=(n_tok // tt,),
        in_specs=[pl.BlockSpec((ROUTE_SUBLANES, tt), lambda i: (0, i), memory_space=pltpu.SMEM),
                  pl.BlockSpec(memory_space=pltpu.SMEM),
                  pl.BlockSpec((tt, d), lambda i: (i, 0))],
        out_specs=pl.BlockSpec(memory_space=pl.ANY),
        out_shape=jax.ShapeDtypeStruct((n_rows, d), F32),
        scratch_shapes=[pltpu.VMEM((EXPERT_ROWS, d), F32), pltpu.SemaphoreType.DMA(())],
        compiler_params=_params(1),
        name="moe_dispatch",
    )(dest_t, pad_end, h2)


def _expert_kernel(be_ref, nu_ref, x_ref, wgu_ref, bgu_ref, wd_ref, bd_ref, o_ref):
    de = wd_ref.shape[1]

    @pl.when(pl.program_id(0) < nu_ref[0])
    def _():
        gu = _dot(x_ref[...].astype(BF16), wgu_ref[0]) + bgu_ref[0]
        gate = jnp.minimum(gu[:, :de], SWIGLU_LIMIT)
        up = jnp.clip(gu[:, de:], -SWIGLU_LIMIT, SWIGLU_LIMIT)
        glu = gate * jax.nn.sigmoid(SWIGLU_ALPHA * gate)
        o_ref[...] = _dot(((up + 1.0) * glu).astype(BF16), wd_ref[0]) + bd_ref[0]

    @pl.when(pl.program_id(0) >= nu_ref[0])
    def _():
        o_ref[...] = jnp.zeros_like(o_ref)


def _experts(block_expert, n_used, xs, w_gu, b_gu, w_d, b_d):
    n_rows, d = xs.shape
    n_blocks = n_rows // EXPERT_ROWS
    de = w_d.shape[1]
    rows = pl.BlockSpec((EXPERT_ROWS, d), lambda i, be, nu: (jnp.minimum(i, nu[0] - 1), 0))
    per_expert = lambda *dims: pl.BlockSpec((1,) + dims, lambda i, be, nu: (be[i], 0, 0))
    return pl.pallas_call(
        _expert_kernel,
        grid_spec=pltpu.PrefetchScalarGridSpec(
            num_scalar_prefetch=2,
            grid=(n_blocks,),
            in_specs=[rows, per_expert(d, 2 * de), per_expert(1, 2 * de), per_expert(de, d), per_expert(1, d)],
            out_specs=pl.BlockSpec((EXPERT_ROWS, d), lambda i, be, nu: (i, 0)),
        ),
        out_shape=jax.ShapeDtypeStruct((n_rows, d), F32),
        compiler_params=_params(1),
        name="moe_experts",
    )(block_expert, n_used, xs, w_gu, b_gu, w_d, b_d)


def _combine_kernel(dest_ref, gate_ref, x1_ref, mod_ref, g_ref, ys_ref, o_ref, buf_ref, sem):
    tt = x1_ref.shape[0]

    def row_copy(t, k):
        return pltpu.make_async_copy(ys_ref.at[pl.ds(dest_ref[k, t], 1)], buf_ref.at[k, pl.ds(t, 1)], sem)

    def issue(t, carry):
        for k in range(TOP_K):
            row_copy(t, k).start()
        return carry

    lax.fori_loop(0, tt, issue, 0)

    def drain(t, carry):
        for k in range(TOP_K):
            row_copy(t, k).wait()
        return carry

    lax.fori_loop(0, tt, drain, 0)

    eye = lax.broadcasted_iota(I32, (tt, tt), 0) == lax.broadcasted_iota(I32, (tt, tt), 1)
    gates = gate_ref[...]
    moe = jnp.zeros(x1_ref.shape, F32)
    for k in range(TOP_K):
        col = jnp.sum(jnp.where(eye, gates[k:k + 1, :], 0.0), axis=1, keepdims=True)
        moe = moe + col * buf_ref[k]
    x2 = x1_ref[...] + mod_ref[0, 5:6, :] * moe
    o_ref[...] = _rms(x2) * g_ref[...]


def _combine(dest_t, gate_t, x1, mod3, g, ys, seq):
    n_tok, d = x1.shape
    tt = _tile(seq, COMBINE_TOKENS)
    per_seq = seq // tt
    return pl.pallas_call(
        _combine_kernel,
        grid=(n_tok // tt,),
        in_specs=[pl.BlockSpec((ROUTE_SUBLANES, tt), lambda i: (0, i), memory_space=pltpu.SMEM),
                  pl.BlockSpec((ROUTE_SUBLANES, tt), lambda i: (0, i)),
                  pl.BlockSpec((tt, d), lambda i: (i, 0)),
                  pl.BlockSpec((1, 6, d), lambda i: (i // per_seq, 0, 0)),
                  pl.BlockSpec((1, d), lambda i: (0, 0)),
                  pl.BlockSpec(memory_space=pl.ANY)],
        out_specs=pl.BlockSpec((tt, d), lambda i: (i, 0)),
        out_shape=jax.ShapeDtypeStruct((n_tok, d), F32),
        scratch_shapes=[pltpu.VMEM((TOP_K, tt, d), F32), pltpu.SemaphoreType.DMA(())],
        compiler_params=_params(1),
        name="moe_combine",
    )(dest_t, gate_t, x1, mod3, g, ys)


def kernel(x, c, w_ada, b_ada, norm_mix_g, w_in, b_in, hy_conv_w, hy_conv_b, filt_w1, filt_b1, filt_w2, filt_b2, filt_w3, filt_b3, filt_w4, filt_freq, hy_d_skip, lambda_qk, attn_subln_g, w_hy_out, w_attn_out, w_mix_out, norm_ffn_g, w_router, b_router, w_gate_up, b_gate_up, w_down, b_down, final_norm_g):
    bsz, seq, d = x.shape
    depth = w_ada.shape[0]
    n_tok = bsz * seq
    row2 = lambda a: a.reshape(1, -1)

    cos_t, sin_t = _dft_tables(seq)
    cos_b = jnp.asarray(cos_t).astype(BF16)
    sin_b = jnp.asarray(sin_t).astype(BF16)
    z = _filter_features(seq)
    min_decay = math.log(DECAY_TARGET) / SLOW_DECAY_PCT
    max_decay = math.log(DECAY_TARGET) / FAST_DECAY_PCT
    absdelta = jnp.abs(jnp.linspace(min_decay, max_decay, HYENA_WIDTH, dtype=F32))[None, :]

    n_blocks = -(-n_tok * TOP_K // EXPERT_ROWS) + N_EXPERTS
    n_rows = n_blocks * EXPERT_ROWS

    assert depth == 1, "LAMBDA_INIT is the first layer's"
    for l in range(depth):
        mod3 = _adaln_mod(c, w_ada[l], row2(b_ada[l])).reshape(bsz, 6, d)

        u_hy, q, k, v, sg = _in_proj(x, mod3, row2(norm_mix_g[l]), w_in[l].astype(BF16), row2(b_in[l]))

        w1p = jnp.pad(filt_w1[l], ((0, 128 - filt_w1.shape[1]), (0, 0)))
        kr, kq = _hyena_filters(z, w1p, row2(filt_b1[l]), filt_w2[l], row2(filt_b2[l]), filt_w3[l],
                                row2(filt_b3[l]), filt_w4[l], row2(filt_freq[l]), absdelta, cos_b, sin_b)
        y_hy = _hyena(u_hy, hy_conv_w[l], row2(hy_conv_b[l]), row2(hy_d_skip[l]), cos_b, sin_b, kr, kq)

        y_at = _diff_attention(q, k, v, lambda_qk[l], row2(attn_subln_g[l]))

        x1, h2, logits_t = _merge(
            y_hy, y_at, sg, x, mod3, w_hy_out[l].astype(BF16), w_attn_out[l].astype(BF16),
            w_mix_out[l].astype(BF16), row2(norm_ffn_g[l]), w_router[l].T, b_router[l].reshape(-1, 1))

        idx_t, gate_t, rank_t, counts = _route(logits_t)
        counts = counts[:, 0].astype(I32)
        padded = ((counts + EXPERT_ROWS - 1) // EXPERT_ROWS) * EXPERT_ROWS
        pad_end = jnp.cumsum(padded)
        pad_start = pad_end - padded
        dest_t = _dest_rows(idx_t, rank_t, jnp.broadcast_to(pad_start[:, None], (N_EXPERTS, 128)))
        block_start = jnp.arange(n_blocks, dtype=I32) * EXPERT_ROWS
        block_expert = jnp.minimum(jnp.searchsorted(pad_end, block_start, side="right"), N_EXPERTS - 1).astype(I32)
        n_used = (pad_end[-1:] // EXPERT_ROWS).astype(I32)

        xs = _dispatch(dest_t, pad_end.astype(I32), h2.reshape(n_tok, d), n_rows)
        ys = _experts(block_expert, n_used, xs, w_gate_up[l].astype(BF16), b_gate_up[l][:, None, :],
                      w_down[l].astype(BF16), b_down[l][:, None, :])
        x = _combine(dest_t, gate_t, x1.reshape(n_tok, d), mod3, row2(final_norm_g), ys, seq).reshape(bsz, seq, d)
    return x
```

```python
import functools
import math

import numpy as np
import jax
import jax.numpy as jnp
from jax import lax
from jax.experimental import pallas as pl
from jax.experimental.pallas import tpu as pltpu

F32 = jnp.float32
BF16 = jnp.bfloat16
I32 = jnp.int32

EPS = 1e-5
HYENA_WIDTH = 512
FILTER_BANDS = 8
DECAY_TARGET = 1e-2
FAST_DECAY_PCT = 0.3
SLOW_DECAY_PCT = 1.5
ATTN_HEADS = 4
ATTN_HEAD_DIM = 64
ATTN_V_DIM = 2 * ATTN_HEAD_DIM
ATTN_WIDTH = ATTN_HEADS * ATTN_V_DIM
N_EXPERTS = 32
TOP_K = 4
SWIGLU_LIMIT = 7.0
SWIGLU_ALPHA = 1.702
LAMBDA_INIT = 0.8 - 0.6 * math.exp(-0.3 * 0)

IN_PROJ_ROWS = 512
ATTN_Q_ROWS = 512
MERGE_ROWS = 512
ROUTE_TOKENS = 512
DEST_TOKENS = 2048
DISPATCH_TOKENS = 256
EXPERT_ROWS = 512
COMBINE_TOKENS = 256
ROUTE_SUBLANES = 8
ROW_TILE_SUBLANES = 8
LANES = 128

VMEM_LIMIT_BYTES = 56 * 1024 * 1024


def _tile(n, t):
    t = min(n, t)
    assert n % t == 0, (n, t)
    return t


def _params(n_axes):
    return pltpu.CompilerParams(
        dimension_semantics=("arbitrary",) * n_axes, vmem_limit_bytes=VMEM_LIMIT_BYTES
    )


def _split_bf16(a):
    hi = a.astype(BF16)
    lo = (a - hi.astype(F32)).astype(BF16)
    return hi, lo


_NN = (((1,), (0,)), ((), ()))
_NT = (((1,), (1,)), ((), ()))


def _dot3(a, b, dims=_NN):
    ah, al = _split_bf16(a)
    bh, bl = _split_bf16(b)
    d = lambda x, y: lax.dot_general(x, y, dims, preferred_element_type=F32)
    return d(ah, bh) + d(ah, bl) + d(al, bh)


def _dot(a, b):
    return jnp.dot(a, b, preferred_element_type=F32)


def _rms(x):
    return x * lax.rsqrt(jnp.mean(x * x, axis=-1, keepdims=True) + EPS)


def _mod_kernel(c_ref, w_ref, b_ref, o_ref):
    c = c_ref[...]
    o_ref[...] = _dot3(c * jax.nn.sigmoid(c), w_ref[...]) + b_ref[...]


def _adaln_mod(c, w, b):
    bsz, d = c.shape
    n = w.shape[1]
    tn = _tile(n, 1536)
    return pl.pallas_call(
        _mod_kernel,
        grid=(n // tn,),
        in_specs=[
            pl.BlockSpec((bsz, d), lambda j: (0, 0)),
            pl.BlockSpec((d, tn), lambda j: (0, j)),
            pl.BlockSpec((1, tn), lambda j: (0, j)),
        ],
        out_specs=pl.BlockSpec((bsz, tn), lambda j: (0, j)),
        out_shape=jax.ShapeDtypeStruct((bsz, n), F32),
        compiler_params=_params(1),
        name="adaln_mod",
    )(c, w, b)


def _in_proj_kernel(x_ref, mod_ref, g_ref, w_ref, b_ref, uhy_ref, q_ref, k_ref, v_ref, sg_ref, *, cols):
    x = x_ref[0]
    h = _rms(x) * g_ref[...]
    h = h * (1.0 + mod_ref[0, 1:2, :]) + mod_ref[0, 0:1, :]
    hb = h.astype(BF16)

    def proj(lo, hi):
        return _dot(hb, w_ref[:, lo:hi]) + b_ref[:, lo:hi]

    c0, c1, c2, c3, c4 = cols
    uhy_ref[0] = proj(0, c0).astype(BF16)
    q_ref[0] = (proj(c0, c1) * (1.0 / math.sqrt(ATTN_HEAD_DIM))).astype(BF16)
    k_ref[0] = proj(c1, c2).astype(BF16)
    v_ref[0] = proj(c2, c3).astype(BF16)
    sg_ref[0] = jax.nn.sigmoid(proj(c3, c4)).astype(BF16)


def _in_proj(x, mod3, g, w_bf16, b):
    bsz, seq, d = x.shape
    n = w_bf16.shape[1]
    hw3 = 3 * HYENA_WIDTH
    cols = (hw3, hw3 + ATTN_WIDTH, hw3 + 2 * ATTN_WIDTH, hw3 + 3 * ATTN_WIDTH, n)
    tm = _tile(seq, IN_PROJ_ROWS)
    row = lambda width: pl.BlockSpec((1, tm, width), lambda bi, i: (bi, i, 0))
    out = lambda width: jax.ShapeDtypeStruct((bsz, seq, width), BF16)
    return pl.pallas_call(
        functools.partial(_in_proj_kernel, cols=cols),
        grid=(bsz, seq // tm),
        in_specs=[
            row(d),
            pl.BlockSpec((1, 6, d), lambda bi, i: (bi, 0, 0)),
            pl.BlockSpec((1, d), lambda bi, i: (0, 0)),
            pl.BlockSpec((d, n), lambda bi, i: (0, 0)),
            pl.BlockSpec((1, n), lambda bi, i: (0, 0)),
        ],
        out_specs=[row(hw3), row(ATTN_WIDTH), row(ATTN_WIDTH), row(ATTN_WIDTH), row(2 * d)],
        out_shape=[out(hw3), out(ATTN_WIDTH), out(ATTN_WIDTH), out(ATTN_WIDTH), out(2 * d)],
        compiler_params=_params(2),
        name="in_proj",
    )(x, mod3, g, w_bf16, b)


@functools.lru_cache(maxsize=None)
def _dft_tables(seq):
    n_fft = 2 * seq
    idx = np.arange(seq, dtype=np.int64)
    ang = 2.0 * np.pi * ((idx[:, None] * idx[None, :]) % n_fft).astype(np.float64) / n_fft
    return np.cos(ang).astype(np.float32), np.sin(ang).astype(np.float32)


def _filter_features(seq):
    t = jnp.linspace(0.0, 1.0, seq, dtype=F32)[:, None]
    w = 2.0 * math.pi * jnp.arange(seq, dtype=F32)[:, None] / seq
    fr = jnp.linspace(1e-4, FILTER_BANDS - 1, FILTER_BANDS, dtype=F32)[None, :]
    z = jnp.concatenate([t, jnp.cos(fr * w), -jnp.sin(fr * w)], axis=-1)
    return jnp.pad(z, ((0, 0), (0, 128 - z.shape[1])))


def _alternating(rows):
    return jnp.where((rows & 1) == 0, 1.0, -1.0).astype(F32)


def _filter_kernel(z_ref, w1_ref, b1_ref, w2_ref, b2_ref, w3_ref, b3_ref, w4_ref, fq_ref, ad_ref,
                   cos_ref, sin_ref, kr_ref, kq_ref):
    seq = z_ref.shape[0]
    width = ad_ref.shape[1]
    fq = fq_ref[...]
    z = z_ref[...]
    h = jnp.sin(fq * (_dot3(z, w1_ref[...]) + b1_ref[...]))
    h = jnp.sin(fq * (_dot3(h, w2_ref[...]) + b2_ref[...]))
    h = jnp.sin(fq * (_dot3(h, w3_ref[...]) + b3_ref[...]))
    k = _dot3(h, w4_ref[...])
    decay = jnp.exp(-z[:, 0:1] * ad_ref[...])
    rows = lax.broadcasted_iota(I32, (seq, 1), 0)
    first = rows == 0
    k_fwd = k[:, :width] * decay
    k_bwd = jnp.where(first, 0.0, k[:, width:] * decay)
    k_sum = k_fwd + k_bwd
    k_dif = k_bwd - k_fwd
    kr = _dot(cos_ref[...], k_sum.astype(BF16))
    kq = _dot(sin_ref[...], k_dif.astype(BF16))
    nyquist = jnp.sum(_alternating(rows) * k_sum, axis=0, keepdims=True)
    kq = jnp.where(first, nyquist, kq)
    scale = jnp.where(first, 0.5 / seq, 1.0 / seq)
    kr_ref[...] = kr * scale
    kq_ref[...] = kq * scale


def _hyena_filters(z, w1, b1, w2, b2, w3, b3, w4, fq, absdelta, cos_b, sin_b):
    seq = z.shape[0]
    width = absdelta.shape[1]
    args = (z, w1, b1, w2, b2, w3, b3, w4, fq, absdelta, cos_b, sin_b)
    full = lambda a: pl.BlockSpec(a.shape, lambda i: (0,) * a.ndim)
    return pl.pallas_call(
        _filter_kernel,
        grid=(1,),
        in_specs=[full(a) for a in args],
        out_specs=[pl.BlockSpec((seq, width), lambda i: (0, 0))] * 2,
        out_shape=[jax.ShapeDtypeStruct((seq, width), F32)] * 2,
        compiler_params=_params(1),
        name="hyena_filters",
    )(*args)


def _hyena_kernel(x0_ref, x1_ref, v_ref, w0_ref, w1_ref, wv_ref, b0_ref, b1_ref, bv_ref, dskip_ref,
                  cos_ref, sin_ref, kr_ref, kq_ref, o_ref):
    seq = x0_ref.shape[1]
    rows = lax.broadcasted_iota(I32, (seq, 1), 0)
    first = rows == 0
    last = rows == seq - 1
    alt = _alternating(rows)

    def short_conv(u_ref, w_ref, b_ref):
        u = u_ref[0].astype(F32)
        prev = jnp.where(first, 0.0, pltpu.roll(u, 1, 0))
        nxt = jnp.where(last, 0.0, pltpu.roll(u, seq - 1, 0))
        w = w_ref[...]
        return prev * w[0:1] + u * w[1:2] + nxt * w[2:3] + b_ref[...]

    x0 = short_conv(x0_ref, w0_ref, b0_ref)
    x1 = short_conv(x1_ref, w1_ref, b1_ref)
    v = short_conv(v_ref, wv_ref, bv_ref) * x1
    vb = v.astype(BF16)
    cos_m = cos_ref[...]
    sin_m = sin_ref[...]
    vr = _dot(cos_m, vb)
    vq = jnp.where(first, jnp.sum(alt * v, axis=0, keepdims=True), _dot(sin_m, vb))
    kr = kr_ref[...]
    kq = kq_ref[...]
    yr = jnp.where(first, vr * kr, vr * kr + vq * kq)
    yq = jnp.where(first, vq * kq, vq * kr - vr * kq)
    y = _dot(cos_m, yr.astype(BF16)) + _dot(sin_m, yq.astype(BF16)) + alt * yq[0:1, :]
    o_ref[0] = ((y + v * dskip_ref[...]) * x0).astype(BF16)


def _hyena(u_hy, conv_w, conv_b, d_skip, cos_b, sin_b, kr, kq):
    bsz, seq, _ = u_hy.shape
    width = d_skip.shape[1]
    cw = _tile(width, 256)
    nc = width // cw
    u_spec = lambda part: pl.BlockSpec((1, seq, cw), lambda c, b: (b, 0, part * nc + c))
    w_spec = lambda part: pl.BlockSpec((3, cw), lambda c, b: (0, part * nc + c))
    b_spec = lambda part: pl.BlockSpec((1, cw), lambda c, b: (0, part * nc + c))
    full = pl.BlockSpec((seq, seq), lambda c, b: (0, 0))
    k_spec = pl.BlockSpec((seq, cw), lambda c, b: (0, c))
    return pl.pallas_call(
        _hyena_kernel,
        grid=(nc, bsz),
        in_specs=[u_spec(0), u_spec(1), u_spec(2), w_spec(0), w_spec(1), w_spec(2),
                  b_spec(0), b_spec(1), b_spec(2), pl.BlockSpec((1, cw), lambda c, b: (0, c)),
                  full, full, k_spec, k_spec],
        out_specs=pl.BlockSpec((1, seq, cw), lambda c, b: (b, 0, c)),
        out_shape=jax.ShapeDtypeStruct((bsz, seq, width), BF16),
        compiler_params=_params(2),
        name="hyena",
    )(u_hy, u_hy, u_hy, conv_w, conv_w, conv_w, conv_b, conv_b, conv_b, d_skip, cos_b, sin_b, kr, kq)


def _attn_kernel(q_ref, k_ref, v_ref, lq_ref, g_ref, o_ref, bias_ref):
    head = pl.program_id(0)
    qi = pl.program_id(1)
    tq = q_ref.shape[1]
    seq = k_ref.shape[1]

    @pl.when(pl.program_id(2) == 0)
    def _():
        slope = jnp.where(head == 0, 0.25, jnp.where(head == 1, 0.0625, jnp.where(head == 2, 0.015625, 0.00390625)))
        i = lax.broadcasted_iota(I32, (tq, seq), 0) + qi * tq
        j = lax.broadcasted_iota(I32, (tq, seq), 1)
        bias_ref[...] = jnp.abs(i - j).astype(F32) * (-slope)

    q = q_ref[0]
    k = k_ref[0]
    v = v_ref[0]
    lane = lax.broadcasted_iota(I32, (1, ATTN_V_DIM), 1)
    zero = jnp.zeros_like(q)

    def softmax_av(qm):
        s = lax.dot_general(qm, k, _NT, preferred_element_type=F32) + bias_ref[...]
        p = jnp.exp(s - jnp.max(s, axis=-1, keepdims=True))
        denom = jnp.sum(p, axis=-1, keepdims=True)
        return _dot(p.astype(BF16), v) / denom

    o1 = softmax_av(jnp.where(lane < ATTN_HEAD_DIM, q, zero))
    o2 = softmax_av(jnp.where(lane >= ATTN_HEAD_DIM, q, zero))
    lq = lq_ref[...]
    lam = (jnp.exp(jnp.sum(lq[0:1] * lq[1:2], axis=-1, keepdims=True))
           - jnp.exp(jnp.sum(lq[2:3] * lq[3:4], axis=-1, keepdims=True)) + LAMBDA_INIT)
    o = o1 - lam * o2
    o_ref[0] = (_rms(o) * g_ref[...] * (1.0 - LAMBDA_INIT)).astype(BF16)


def _diff_attention(q, k, v, lambda_qk, subln_g):
    bsz, seq, _ = q.shape
    tq = _tile(seq, ATTN_Q_ROWS)
    kv_spec = pl.BlockSpec((1, seq, ATTN_V_DIM), lambda h, i, b: (b, 0, h))
    q_spec = pl.BlockSpec((1, tq, ATTN_V_DIM), lambda h, i, b: (b, i, h))
    return pl.pallas_call(
        _attn_kernel,
        grid=(ATTN_HEADS, seq // tq, bsz),
        in_specs=[q_spec, kv_spec, kv_spec,
                  pl.BlockSpec(lambda_qk.shape, lambda h, i, b: (0, 0)),
                  pl.BlockSpec(subln_g.shape, lambda h, i, b: (0, 0))],
        out_specs=q_spec,
        out_shape=jax.ShapeDtypeStruct((bsz, seq, ATTN_WIDTH), BF16),
        scratch_shapes=[pltpu.VMEM((tq, seq), F32)],
        compiler_params=_params(3),
        name="diff_attention",
    )(q, k, v, lambda_qk, subln_g)


def _merge_kernel(yhy_ref, yat_ref, sg_ref, x_ref, mod_ref, why_ref, wat_ref, wmix_ref, g_ref,
                  wr_ref, br_ref, x1_ref, h2_ref, lg_ref):
    d = x_ref.shape[2]
    sg = sg_ref[0]
    merged = (sg[:, :d].astype(F32) * _dot(yhy_ref[0], why_ref[...])
              + sg[:, d:].astype(F32) * _dot(yat_ref[0], wat_ref[...]))
    x1 = x_ref[0] + mod_ref[0, 2:3, :] * _dot(merged.astype(BF16), wmix_ref[...])
    x1_ref[0] = x1
    h2 = _rms(x1) * g_ref[...]
    h2 = h2 * (1.0 + mod_ref[0, 4:5, :]) + mod_ref[0, 3:4, :]
    h2_ref[0] = h2
    lg_ref[0] = _dot3(wr_ref[...], h2, _NT) + br_ref[...]


def _merge(y_hy, y_at, sg, x, mod3, w_hy, w_at, w_mix, g, w_router_t, b_router):
    bsz, seq, d = x.shape
    tm = _tile(seq, MERGE_ROWS)
    row = lambda width: pl.BlockSpec((1, tm, width), lambda b, i: (b, i, 0))
    full = lambda a: pl.BlockSpec(a.shape, lambda b, i: (0,) * a.ndim)
    return pl.pallas_call(
        _merge_kernel,
        grid=(bsz, seq // tm),
        in_specs=[row(y_hy.shape[2]), row(y_at.shape[2]), row(2 * d), row(d),
                  pl.BlockSpec((1, 6, d), lambda b, i: (b, 0, 0)),
                  full(w_hy), full(w_at), full(w_mix), full(g), full(w_router_t), full(b_router)],
        out_specs=[row(d), row(d), pl.BlockSpec((1, N_EXPERTS, tm), lambda b, i: (b, 0, i))],
        out_shape=[jax.ShapeDtypeStruct((bsz, seq, d), F32), jax.ShapeDtypeStruct((bsz, seq, d), F32),
                   jax.ShapeDtypeStruct((bsz, N_EXPERTS, seq), F32)],
        compiler_params=_params(2),
        name="merge_router",
    )(y_hy, y_at, sg, x, mod3, w_hy, w_at, w_mix, g, w_router_t, b_router)


def _route_kernel(lg_ref, idx_ref, gate_ref, rank_ref, cnt_ref, carry_ref):
    step = pl.program_id(0) * pl.num_programs(1) + pl.program_id(1)

    @pl.when(step == 0)
    def _():
        carry_ref[...] = jnp.zeros_like(carry_ref)

    work = lg_ref[0]
    n_exp, tr = work.shape
    e_iota = lax.broadcasted_iota(I32, (n_exp, tr), 0)
    vals, idxs = [], []
    for _ in range(TOP_K):
        m = jnp.max(work, axis=0, keepdims=True)
        ik = jnp.min(jnp.where(work == m, e_iota, n_exp), axis=0, keepdims=True)
        vals.append(m)
        idxs.append(ik)
        work = jnp.where(e_iota == ik, -jnp.inf, work)
    ex = [jnp.exp(val - vals[0]) for val in vals]
    denom = ex[0] + ex[1] + ex[2] + ex[3]
    sel = jnp.zeros((n_exp, tr), F32)
    for ik in idxs:
        sel = sel + jnp.where(e_iota == ik, 1.0, 0.0)
    upper = jnp.where(lax.broadcasted_iota(I32, (tr, tr), 0) < lax.broadcasted_iota(I32, (tr, tr), 1), 1.0, 0.0)
    rank_all = _dot(sel.astype(BF16), upper.astype(BF16)) + carry_ref[:, 0:1]
    pad = ROUTE_SUBLANES - TOP_K
    ranks = [jnp.sum(jnp.where(e_iota == ik, rank_all, 0.0), axis=0, keepdims=True) for ik in idxs]
    idx_ref[...] = jnp.concatenate(idxs + [jnp.zeros((pad, tr), I32)], axis=0)
    gate_ref[...] = jnp.concatenate([e / denom for e in ex] + [jnp.zeros((pad, tr), F32)], axis=0)
    rank_ref[...] = jnp.concatenate(ranks + [jnp.zeros((pad, tr), F32)], axis=0).astype(I32)
    carry_ref[...] = carry_ref[...] + jnp.sum(sel, axis=1, keepdims=True)
    cnt_ref[...] = carry_ref[...]


def _route(logits_t):
    bsz, n_exp, seq = logits_t.shape
    tr = _tile(seq, ROUTE_TOKENS)
    nt = seq // tr
    tok = pl.BlockSpec((ROUTE_SUBLANES, tr), lambda b, i: (0, b * nt + i))
    shape = lambda dt: jax.ShapeDtypeStruct((ROUTE_SUBLANES, bsz * seq), dt)
    return pl.pallas_call(
        _route_kernel,
        grid=(bsz, nt),
        in_specs=[pl.BlockSpec((1, n_exp, tr), lambda b, i: (b, 0, i))],
        out_specs=[tok, tok, tok, pl.BlockSpec((n_exp, 128), lambda b, i: (0, 0))],
        out_shape=[shape(I32), shape(F32), shape(I32), jax.ShapeDtypeStruct((n_exp, 128), F32)],
        scratch_shapes=[pltpu.VMEM((n_exp, 128), F32)],
        compiler_params=_params(2),
        name="route_topk",
    )(logits_t)


def _dest_kernel(idx_ref, rank_ref, start_ref, dest_ref):
    idx = idx_ref[...]
    n_exp = start_ref.shape[0]
    tr = idx.shape[1]
    e_iota = lax.broadcasted_iota(I32, (n_exp, tr), 0)
    start = start_ref[:, 0:1]
    rows = [jnp.sum(jnp.where(e_iota == idx[k:k + 1, :], start, 0), axis=0, keepdims=True) for k in range(TOP_K)]
    rows.append(jnp.zeros((ROUTE_SUBLANES - TOP_K, tr), I32))
    dest_ref[...] = rank_ref[...] + jnp.concatenate(rows, axis=0)


def _dest_rows(idx_t, rank_t, pad_start):
    n_tok = idx_t.shape[1]
    tr = _tile(n_tok, DEST_TOKENS)
    tok = pl.BlockSpec((ROUTE_SUBLANES, tr), lambda i: (0, i))
    return pl.pallas_call(
        _dest_kernel,
        grid=(n_tok // tr,),
        in_specs=[tok, tok, pl.BlockSpec(pad_start.shape, lambda i: (0, 0))],
        out_specs=tok,
        out_shape=jax.ShapeDtypeStruct(idx_t.shape, I32),
        compiler_params=_params(1),
        name="dest_rows",
    )(idx_t, rank_t, pad_start)


def _to_row_tiles(dst_ref, x):
    for s in range(ROW_TILE_SUBLANES):
        dst_ref[:, s, :] = x[:, s * LANES:(s + 1) * LANES]


def _from_row_tiles(src_ref):
    return jnp.concatenate([src_ref[:, s, :] for s in range(ROW_TILE_SUBLANES)], axis=1)


def _dispatch_kernel(dest_ref, pend_ref, h_ref, xs_ref, rows_ref, zeros_ref, sems):
    step = pl.program_id(0)
    last = pl.num_programs(0) - 1
    tt = h_ref.shape[0]
    bm = zeros_ref.shape[0]
    slot = step % 2

    @pl.when(step == 0)
    def _():
        zeros_ref[...] = jnp.zeros_like(zeros_ref)

        def zero_copy(start):
            return pltpu.make_async_copy(zeros_ref, xs_ref.at[pl.ds(start, bm)], sems.at[0])

        for e in range(N_EXPERTS):
            zero_copy(jnp.maximum(pend_ref[e] - bm, 0)).start()
        for e in range(N_EXPERTS):
            zero_copy(0).wait()

        def zero_tail(j, carry):
            zero_copy(j * bm).start()
            zero_copy(0).wait()
            return carry

        lax.fori_loop(pend_ref[N_EXPERTS - 1] // bm, xs_ref.shape[0] // bm, zero_tail, 0)

    _to_row_tiles(rows_ref.at[slot], h_ref[...])

    def issue(t, carry):
        for k in range(TOP_K):
            pltpu.make_async_copy(rows_ref.at[slot, t], xs_ref.at[dest_ref[t * TOP_K + k]],
                                  sems.at[slot]).start(priority=k % 2)
        return carry

    lax.fori_loop(0, tt, issue, 0, unroll=4)

    def drain(sl):
        def body(t, carry):
            for _ in range(TOP_K):
                pltpu.make_async_copy(rows_ref.at[sl, 0], xs_ref.at[0], sems.at[sl]).wait()
            return carry

        lax.fori_loop(0, tt, body, 0, unroll=4)

    @pl.when(step > 0)
    def _():
        drain(1 - slot)

    @pl.when(step == last)
    def _():
        drain(slot)


def _dispatch(dest_flat, pad_end, h2, n_rows):
    n_tok, d = h2.shape
    assert d == ROW_TILE_SUBLANES * LANES
    tt = _tile(n_tok, DISPATCH_TOKENS)
    tile = (ROW_TILE_SUBLANES, LANES)
    return pl.pallas_call(
        _dispatch_kernel,
        grid=(n_tok // tt,),
        in_specs=[pl.BlockSpec((tt * TOP_K,), lambda i: (i,), memory_space=pltpu.SMEM),
                  pl.BlockSpec(memory_space=pltpu.SMEM),
                  pl.BlockSpec((tt, d), lambda i: (i, 0))],
        out_specs=pl.BlockSpec(memory_space=pl.ANY),
        out_shape=jax.ShapeDtypeStruct((n_rows,) + tile, F32),
        scratch_shapes=[pltpu.VMEM((2, tt) + tile, F32), pltpu.VMEM((EXPERT_ROWS,) + tile, F32),
                        pltpu.SemaphoreType.DMA((2,))],
        compiler_params=_params(1),
        name="moe_dispatch",
    )(dest_flat, pad_end, h2)


def _expert_kernel(be_ref, nu_ref, x_ref, wgu_ref, bgu_ref, wd_ref, bd_ref, o_ref, wgu_bf16, wd_bf16):
    step = pl.program_id(0)
    de = wd_ref.shape[1]

    @pl.when(step < nu_ref[0])
    def _():
        @pl.when((step == 0) | (be_ref[step] != be_ref[jnp.maximum(step - 1, 0)]))
        def _():
            wgu_bf16[...] = wgu_ref[0].astype(BF16)
            wd_bf16[...] = wd_ref[0].astype(BF16)

        gu = _dot(_from_row_tiles(x_ref).astype(BF16), wgu_bf16[...]) + bgu_ref[0]
        gate = jnp.minimum(gu[:, :de], SWIGLU_LIMIT)
        up = jnp.clip(gu[:, de:], -SWIGLU_LIMIT, SWIGLU_LIMIT)
        glu = gate * jax.nn.sigmoid(SWIGLU_ALPHA * gate)
        _to_row_tiles(o_ref, _dot(((up + 1.0) * glu).astype(BF16), wd_bf16[...]) + bd_ref[0])

    @pl.when(step >= nu_ref[0])
    def _():
        o_ref[...] = jnp.zeros_like(o_ref)


def _experts(block_expert, n_used, xs, w_gu, b_gu, w_d, b_d):
    n_rows = xs.shape[0]
    n_blocks = n_rows // EXPERT_ROWS
    _, d, de2 = w_gu.shape
    de = w_d.shape[1]
    tile = (ROW_TILE_SUBLANES, LANES)
    rows = pl.BlockSpec((EXPERT_ROWS,) + tile, lambda i, be, nu: (jnp.minimum(i, nu[0] - 1), 0, 0))
    per_expert = lambda *dims: pl.BlockSpec((1,) + dims, lambda i, be, nu: (be[i], 0, 0))
    return pl.pallas_call(
        _expert_kernel,
        grid_spec=pltpu.PrefetchScalarGridSpec(
            num_scalar_prefetch=2,
            grid=(n_blocks,),
            in_specs=[rows, per_expert(d, de2), per_expert(1, de2), per_expert(de, d), per_expert(1, d)],
            out_specs=pl.BlockSpec((EXPERT_ROWS,) + tile, lambda i, be, nu: (i, 0, 0)),
            scratch_shapes=[pltpu.VMEM((d, de2), BF16), pltpu.VMEM((de, d), BF16)],
        ),
        out_shape=jax.ShapeDtypeStruct(xs.shape, F32),
        compiler_params=_params(1),
        name="moe_experts",
    )(block_expert, n_used, xs, w_gu, b_gu, w_d, b_d)


def _combine_kernel(dest_ref, next_ref, gate_ref, x1_ref, mod_ref, g_ref, ys_ref, o_ref, buf_ref, sems):
    step = pl.program_id(0)
    tt = x1_ref.shape[0]
    slot = step % 2

    def issue(idx_ref, sl):
        def body(t, carry):
            for k in range(TOP_K):
                pltpu.make_async_copy(ys_ref.at[idx_ref[t * TOP_K + k]], buf_ref.at[sl, k, t],
                                      sems.at[sl]).start(priority=k % 2)
            return carry

        lax.fori_loop(0, tt, body, 0, unroll=4)

    @pl.when(step == 0)
    def _():
        issue(dest_ref, 0)

    @pl.when(step < pl.num_programs(0) - 1)
    def _():
        issue(next_ref, 1 - slot)

    def drain(t, carry):
        for _ in range(TOP_K):
            pltpu.make_async_copy(ys_ref.at[0], buf_ref.at[slot, 0, 0], sems.at[slot]).wait()
        return carry

    lax.fori_loop(0, tt, drain, 0, unroll=4)

    eye = lax.broadcasted_iota(I32, (tt, tt), 0) == lax.broadcasted_iota(I32, (tt, tt), 1)
    gates = gate_ref[...]
    moe = jnp.zeros(x1_ref.shape, F32)
    for k in range(TOP_K):
        col = jnp.sum(jnp.where(eye, gates[k:k + 1, :], 0.0), axis=1, keepdims=True)
        moe = moe + col * _from_row_tiles(buf_ref.at[slot, k])
    x2 = x1_ref[...] + mod_ref[0, 5:6, :] * moe
    o_ref[...] = _rms(x2) * g_ref[...]


def _combine(dest_flat, gate_t, x1, mod3, g, ys, seq):
    n_tok, d = x1.shape
    tt = _tile(seq, COMBINE_TOKENS)
    per_seq = seq // tt
    n_steps = n_tok // tt
    return pl.pallas_call(
        _combine_kernel,
        grid=(n_steps,),
        in_specs=[pl.BlockSpec((tt * TOP_K,), lambda i: (i,), memory_space=pltpu.SMEM),
                  pl.BlockSpec((tt * TOP_K,), lambda i: (jnp.minimum(i + 1, n_steps - 1),), memory_space=pltpu.SMEM),
                  pl.BlockSpec((ROUTE_SUBLANES, tt), lambda i: (0, i)),
                  pl.BlockSpec((tt, d), lambda i: (i, 0)),
                  pl.BlockSpec((1, 6, d), lambda i: (i // per_seq, 0, 0)),
                  pl.BlockSpec((1, d), lambda i: (0, 0)),
                  pl.BlockSpec(memory_space=pl.ANY)],
        out_specs=pl.BlockSpec((tt, d), lambda i: (i, 0)),
        out_shape=jax.ShapeDtypeStruct((n_tok, d), F32),
        scratch_shapes=[pltpu.VMEM((2, TOP_K, tt, ROW_TILE_SUBLANES, LANES), F32), pltpu.SemaphoreType.DMA((2,))],
        compiler_params=_params(1),
        name="moe_combine",
    )(dest_flat, dest_flat, gate_t, x1, mod3, g, ys)


def kernel(x, c, w_ada, b_ada, norm_mix_g, w_in, b_in, hy_conv_w, hy_conv_b, filt_w1, filt_b1, filt_w2, filt_b2, filt_w3, filt_b3, filt_w4, filt_freq, hy_d_skip, lambda_qk, attn_subln_g, w_hy_out, w_attn_out, w_mix_out, norm_ffn_g, w_router, b_router, w_gate_up, b_gate_up, w_down, b_down, final_norm_g):
    bsz, seq, d = x.shape
    depth = w_ada.shape[0]
    n_tok = bsz * seq
    row2 = lambda a: a.reshape(1, -1)

    cos_t, sin_t = _dft_tables(seq)
    cos_b = jnp.asarray(cos_t).astype(BF16)
    sin_b = jnp.asarray(sin_t).astype(BF16)
    z = _filter_features(seq)
    min_decay = math.log(DECAY_TARGET) / SLOW_DECAY_PCT
    max_decay = math.log(DECAY_TARGET) / FAST_DECAY_PCT
    absdelta = jnp.abs(jnp.linspace(min_decay, max_decay, HYENA_WIDTH, dtype=F32))[None, :]

    n_blocks = -(-n_tok * TOP_K // EXPERT_ROWS) + N_EXPERTS
    n_rows = n_blocks * EXPERT_ROWS

    assert depth == 1, "LAMBDA_INIT is the first layer's"
    for l in range(depth):
        mod3 = _adaln_mod(c, w_ada[l], row2(b_ada[l])).reshape(bsz, 6, d)

        u_hy, q, k, v, sg = _in_proj(x, mod3, row2(norm_mix_g[l]), w_in[l].astype(BF16), row2(b_in[l]))

        w1p = jnp.pad(filt_w1[l], ((0, 128 - filt_w1.shape[1]), (0, 0)))
        kr, kq = _hyena_filters(z, w1p, row2(filt_b1[l]), filt_w2[l], row2(filt_b2[l]), filt_w3[l],
                                row2(filt_b3[l]), filt_w4[l], row2(filt_freq[l]), absdelta, cos_b, sin_b)
        y_hy = _hyena(u_hy, hy_conv_w[l], row2(hy_conv_b[l]), row2(hy_d_skip[l]), cos_b, sin_b, kr, kq)

        y_at = _diff_attention(q, k, v, lambda_qk[l], row2(attn_subln_g[l]))

        x1, h2, logits_t = _merge(
            y_hy, y_at, sg, x, mod3, w_hy_out[l].astype(BF16), w_attn_out[l].astype(BF16),
            w_mix_out[l].astype(BF16), row2(norm_ffn_g[l]), w_router[l].T, b_router[l].reshape(-1, 1))

        idx_t, gate_t, rank_t, counts = _route(logits_t)
        counts = counts[:, 0].astype(I32)
        padded = ((counts + EXPERT_ROWS - 1) // EXPERT_ROWS) * EXPERT_ROWS
        pad_end = jnp.cumsum(padded)
        pad_start = pad_end - padded
        dest_t = _dest_rows(idx_t, rank_t, jnp.broadcast_to(pad_start[:, None], (N_EXPERTS, 128)))
        dest_flat = dest_t[:TOP_K].T.reshape(-1)
        block_start = jnp.arange(n_blocks, dtype=I32) * EXPERT_ROWS
        block_expert = jnp.sum((block_start[:, None] >= pad_end[None, :]).astype(I32), axis=1)
        block_expert = jnp.minimum(block_expert, N_EXPERTS - 1)
        n_used = (pad_end[-1:] // EXPERT_ROWS).astype(I32)

        xs = _dispatch(dest_flat, pad_end.astype(I32), h2.reshape(n_tok, d), n_rows)
        ys = _experts(block_expert, n_used, xs, w_gate_up[l], b_gate_up[l][:, None, :],
                      w_down[l], b_down[l][:, None, :])
        x = _combine(dest_flat, gate_t, x1.reshape(n_tok, d), mod3, row2(final_norm_g), ys, seq).reshape(bsz, seq, d)
    return x
```

```python
import functools
import math

import numpy as np
import jax
import jax.numpy as jnp
from jax import lax
from jax.experimental import pallas as pl
from jax.experimental.pallas import tpu as pltpu

F32 = jnp.float32
BF16 = jnp.bfloat16
I32 = jnp.int32

EPS = 1e-5
HYENA_WIDTH = 512
FILTER_BANDS = 8
DECAY_TARGET = 1e-2
FAST_DECAY_PCT = 0.3
SLOW_DECAY_PCT = 1.5
ATTN_HEADS = 4
ATTN_HEAD_DIM = 64
ATTN_V_DIM = 2 * ATTN_HEAD_DIM
ATTN_WIDTH = ATTN_HEADS * ATTN_V_DIM
N_EXPERTS = 32
TOP_K = 4
SWIGLU_LIMIT = 7.0
SWIGLU_ALPHA = 1.702
LAMBDA_INIT = 0.8 - 0.6 * math.exp(-0.3 * 0)
ALIBI_SPLIT_LOG2 = 6

IN_PROJ_ROWS = 512
ATTN_Q_ROWS = 512
MERGE_ROWS = 512
ROUTE_TOKENS = 512
DEST_TOKENS = 2048
DISPATCH_TOKENS = 256
EXPERT_ROWS = 512
COMBINE_TOKENS = 256
ROUTE_SUBLANES = 8
ROW_TILE_SUBLANES = 8
LANES = 128

VMEM_LIMIT_BYTES = 56 * 1024 * 1024


def _tile(n, t):
    t = min(n, t)
    assert n % t == 0, (n, t)
    return t


def _params(n_axes):
    return pltpu.CompilerParams(
        dimension_semantics=("arbitrary",) * n_axes, vmem_limit_bytes=VMEM_LIMIT_BYTES
    )


def _split_bf16(a):
    hi = a.astype(BF16)
    lo = (a - hi.astype(F32)).astype(BF16)
    return hi, lo


_NN = (((1,), (0,)), ((), ()))
_NT = (((1,), (1,)), ((), ()))


def _dot3(a, b, dims=_NN):
    ah, al = _split_bf16(a)
    bh, bl = _split_bf16(b)
    d = lambda x, y: lax.dot_general(x, y, dims, preferred_element_type=F32)
    return d(ah, bh) + d(ah, bl) + d(al, bh)


def _dot(a, b):
    return jnp.dot(a, b, preferred_element_type=F32)


def _rms(x):
    return x * lax.rsqrt(jnp.mean(x * x, axis=-1, keepdims=True) + EPS)


def _mod_kernel(c_ref, w_ref, b_ref, o_ref):
    c = c_ref[...]
    o_ref[...] = _dot3(c * jax.nn.sigmoid(c), w_ref[...]) + b_ref[...]


def _adaln_mod(c, w, b):
    bsz, d = c.shape
    n = w.shape[1]
    tn = _tile(n, 1536)
    return pl.pallas_call(
        _mod_kernel,
        grid=(n // tn,),
        in_specs=[
            pl.BlockSpec((bsz, d), lambda j: (0, 0)),
            pl.BlockSpec((d, tn), lambda j: (0, j)),
            pl.BlockSpec((1, tn), lambda j: (0, j)),
        ],
        out_specs=pl.BlockSpec((bsz, tn), lambda j: (0, j)),
        out_shape=jax.ShapeDtypeStruct((bsz, n), F32),
        compiler_params=_params(1),
        name="adaln_mod",
    )(c, w, b)


def _alibi_tables(seq):
    slopes = 2.0 ** (-8.0 * jnp.arange(1, ATTN_HEADS + 1, dtype=F32) / ATTN_HEADS)
    pos = jnp.arange(seq, dtype=I32)
    hi = (jnp.right_shift(pos, ALIBI_SPLIT_LOG2) << ALIBI_SPLIT_LOG2).astype(F32)[:, None] * slopes[None, :]
    lo = (pos & ((1 << ALIBI_SPLIT_LOG2) - 1)).astype(F32)[:, None] * slopes[None, :]
    one = jnp.ones_like(hi)
    key4 = jnp.stack([hi, lo, one, one], axis=-1)
    qry4 = jnp.stack([one, one, -hi, -lo], axis=-1)

    def place(vals, first_lane):
        tab = jnp.zeros((seq, ATTN_HEADS, ATTN_V_DIM), F32).at[:, :, first_lane:first_lane + 4].set(vals)
        return tab.reshape(seq, ATTN_WIDTH)

    half = ATTN_HEAD_DIM
    return place(key4, half), place(key4, 0), place(qry4, half), place(qry4, 0)


def _in_proj_kernel(x_ref, mod_ref, g_ref, w_ref, b_ref, kpa_ref, kpb_ref,
                    uhy_ref, q_ref, ka_ref, kb_ref, v_ref, sg_ref, *, cols):
    x = x_ref[0]
    h = _rms(x) * g_ref[...]
    h = h * (1.0 + mod_ref[0, 1:2, :]) + mod_ref[0, 0:1, :]
    hb = h.astype(BF16)

    def proj(lo, hi):
        return _dot(hb, w_ref[:, lo:hi]) + b_ref[:, lo:hi]

    c0, c1, c2, c3, c4 = cols
    uhy_ref[0] = proj(0, c0).astype(BF16)
    q_ref[0] = (proj(c0, c1) * (1.0 / math.sqrt(ATTN_HEAD_DIM))).astype(BF16)
    k = proj(c1, c2).astype(BF16)
    first_half = (lax.broadcasted_iota(I32, (1, ATTN_WIDTH), 1) % ATTN_V_DIM) < ATTN_HEAD_DIM
    ka_ref[0] = jnp.where(first_half, k, kpa_ref[...])
    kb_ref[0] = jnp.where(first_half, kpb_ref[...], k)
    v = proj(c2, c3).astype(BF16)
    ones_col = jnp.where(lax.broadcasted_iota(I32, (v.shape[0], ATTN_V_DIM), 1) == 0, 1.0, 0.0).astype(BF16)
    pieces = []
    for hd in range(ATTN_HEADS):
        pieces += [v[:, hd * ATTN_V_DIM:(hd + 1) * ATTN_V_DIM], ones_col]
    v_ref[0] = jnp.concatenate(pieces, axis=1)
    sg_ref[0] = jax.nn.sigmoid(proj(c3, c4)).astype(BF16)


def _in_proj(x, mod3, g, w_bf16, b, kpos_a, kpos_b):
    bsz, seq, d = x.shape
    n = w_bf16.shape[1]
    hw3 = 3 * HYENA_WIDTH
    cols = (hw3, hw3 + ATTN_WIDTH, hw3 + 2 * ATTN_WIDTH, hw3 + 3 * ATTN_WIDTH, n)
    tm = _tile(seq, IN_PROJ_ROWS)
    row = lambda width: pl.BlockSpec((1, tm, width), lambda bi, i: (bi, i, 0))
    out = lambda width: jax.ShapeDtypeStruct((bsz, seq, width), BF16)
    pos = pl.BlockSpec((tm, ATTN_WIDTH), lambda bi, i: (i, 0))
    widths = (hw3, ATTN_WIDTH, ATTN_WIDTH, ATTN_WIDTH, 2 * ATTN_WIDTH, 2 * d)
    return pl.pallas_call(
        functools.partial(_in_proj_kernel, cols=cols),
        grid=(bsz, seq // tm),
        in_specs=[
            row(d),
            pl.BlockSpec((1, 6, d), lambda bi, i: (bi, 0, 0)),
            pl.BlockSpec((1, d), lambda bi, i: (0, 0)),
            pl.BlockSpec((d, n), lambda bi, i: (0, 0)),
            pl.BlockSpec((1, n), lambda bi, i: (0, 0)),
            pos, pos,
        ],
        out_specs=[row(w) for w in widths],
        out_shape=[out(w) for w in widths],
        compiler_params=_params(2),
        name="in_proj",
    )(x, mod3, g, w_bf16, b, kpos_a, kpos_b)


@functools.lru_cache(maxsize=None)
def _dft_tables(seq):
    n_fft = 2 * seq
    idx = np.arange(seq, dtype=np.int64)
    ang = 2.0 * np.pi * ((idx[:, None] * idx[None, :]) % n_fft).astype(np.float64) / n_fft
    return np.cos(ang).astype(np.float32), np.sin(ang).astype(np.float32)


def _filter_features(seq):
    t = jnp.linspace(0.0, 1.0, seq, dtype=F32)[:, None]
    w = 2.0 * math.pi * jnp.arange(seq, dtype=F32)[:, None] / seq
    fr = jnp.linspace(1e-4, FILTER_BANDS - 1, FILTER_BANDS, dtype=F32)[None, :]
    z = jnp.concatenate([t, jnp.cos(fr * w), -jnp.sin(fr * w)], axis=-1)
    return jnp.pad(z, ((0, 0), (0, 128 - z.shape[1])))


def _alternating(rows):
    return jnp.where((rows & 1) == 0, 1.0, -1.0).astype(F32)


def _filter_kernel(z_ref, w1_ref, b1_ref, w2_ref, b2_ref, w3_ref, b3_ref, w4_ref, fq_ref, ad_ref,
                   cos_ref, sin_ref, kr_ref, kq_ref):
    seq = z_ref.shape[0]
    width = ad_ref.shape[1]
    fq = fq_ref[...]
    z = z_ref[...]
    h = jnp.sin(fq * (_dot3(z, w1_ref[...]) + b1_ref[...]))
    h = jnp.sin(fq * (_dot3(h, w2_ref[...]) + b2_ref[...]))
    h = jnp.sin(fq * (_dot3(h, w3_ref[...]) + b3_ref[...]))
    k = _dot3(h, w4_ref[...])
    decay = jnp.exp(-z[:, 0:1] * ad_ref[...])
    rows = lax.broadcasted_iota(I32, (seq, 1), 0)
    first = rows == 0
    k_fwd = k[:, :width] * decay
    k_bwd = jnp.where(first, 0.0, k[:, width:] * decay)
    k_sum = k_fwd + k_bwd
    k_dif = k_bwd - k_fwd
    kr = _dot(cos_ref[...], k_sum.astype(BF16))
    kq = _dot(sin_ref[...], k_dif.astype(BF16))
    nyquist = jnp.sum(_alternating(rows) * k_sum, axis=0, keepdims=True)
    kq = jnp.where(first, nyquist, kq)
    scale = jnp.where(first, 0.5 / seq, 1.0 / seq)
    kr_ref[...] = kr * scale
    kq_ref[...] = kq * scale


def _hyena_filters(z, w1, b1, w2, b2, w3, b3, w4, fq, absdelta, cos_b, sin_b):
    seq = z.shape[0]
    width = absdelta.shape[1]
    args = (z, w1, b1, w2, b2, w3, b3, w4, fq, absdelta, cos_b, sin_b)
    full = lambda a: pl.BlockSpec(a.shape, lambda i: (0,) * a.ndim)
    return pl.pallas_call(
        _filter_kernel,
        grid=(1,),
        in_specs=[full(a) for a in args],
        out_specs=[pl.BlockSpec((seq, width), lambda i: (0, 0))] * 2,
        out_shape=[jax.ShapeDtypeStruct((seq, width), F32)] * 2,
        compiler_params=_params(1),
        name="hyena_filters",
    )(*args)


def _hyena_kernel(x0_ref, x1_ref, v_ref, w0_ref, w1_ref, wv_ref, b0_ref, b1_ref, bv_ref, dskip_ref,
                  cos_ref, sin_ref, kr_ref, kq_ref, o_ref):
    seq = x0_ref.shape[1]
    rows = lax.broadcasted_iota(I32, (seq, 1), 0)
    first = rows == 0
    last = rows == seq - 1
    alt = _alternating(rows)

    def short_conv(u_ref, w_ref, b_ref):
        u = u_ref[0].astype(F32)
        prev = jnp.where(first, 0.0, pltpu.roll(u, 1, 0))
        nxt = jnp.where(last, 0.0, pltpu.roll(u, seq - 1, 0))
        w = w_ref[...]
        return prev * w[0:1] + u * w[1:2] + nxt * w[2:3] + b_ref[...]

    x0 = short_conv(x0_ref, w0_ref, b0_ref)
    x1 = short_conv(x1_ref, w1_ref, b1_ref)
    v = short_conv(v_ref, wv_ref, bv_ref) * x1
    vb = v.astype(BF16)
    cos_m = cos_ref[...]
    sin_m = sin_ref[...]
    vr = _dot(cos_m, vb)
    vq = jnp.where(first, jnp.sum(alt * v, axis=0, keepdims=True), _dot(sin_m, vb))
    kr = kr_ref[...]
    kq = kq_ref[...]
    yr = jnp.where(first, vr * kr, vr * kr + vq * kq)
    yq = jnp.where(first, vq * kq, vq * kr - vr * kq)
    y = _dot(cos_m, yr.astype(BF16)) + _dot(sin_m, yq.astype(BF16)) + alt * yq[0:1, :]
    o_ref[0] = ((y + v * dskip_ref[...]) * x0).astype(BF16)


def _hyena(u_hy, conv_w, conv_b, d_skip, cos_b, sin_b, kr, kq):
    bsz, seq, _ = u_hy.shape
    width = d_skip.shape[1]
    cw = _tile(width, 256)
    nc = width // cw
    u_spec = lambda part: pl.BlockSpec((1, seq, cw), lambda c, b: (b, 0, part * nc + c))
    w_spec = lambda part: pl.BlockSpec((3, cw), lambda c, b: (0, part * nc + c))
    b_spec = lambda part: pl.BlockSpec((1, cw), lambda c, b: (0, part * nc + c))
    full = pl.BlockSpec((seq, seq), lambda c, b: (0, 0))
    k_spec = pl.BlockSpec((seq, cw), lambda c, b: (0, c))
    return pl.pallas_call(
        _hyena_kernel,
        grid=(nc, bsz),
        in_specs=[u_spec(0), u_spec(1), u_spec(2), w_spec(0), w_spec(1), w_spec(2),
                  b_spec(0), b_spec(1), b_spec(2), pl.BlockSpec((1, cw), lambda c, b: (0, c)),
                  full, full, k_spec, k_spec],
        out_specs=pl.BlockSpec((1, seq, cw), lambda c, b: (b, 0, c)),
        out_shape=jax.ShapeDtypeStruct((bsz, seq, width), BF16),
        compiler_params=_params(2),
        name="hyena",
    )(u_hy, u_hy, u_hy, conv_w, conv_w, conv_w, conv_b, conv_b, conv_b, d_skip, cos_b, sin_b, kr, kq)


def _attn_kernel(q_ref, ka_ref, kb_ref, v_ref, qpa_ref, qpb_ref, lq_ref, g_ref, o_ref, s_ref, bias_ref):
    head = pl.program_id(0)
    qi = pl.program_id(1)
    tq = q_ref.shape[1]
    seq = ka_ref.shape[1]
    nk = seq // tq

    @pl.when(pl.program_id(2) == 0)
    def _():
        slope = jnp.where(head == 0, 0.25, jnp.where(head == 1, 0.0625, jnp.where(head == 2, 0.015625, 0.00390625)))
        r = lax.broadcasted_iota(I32, (tq, tq), 0)
        c = lax.broadcasted_iota(I32, (tq, tq), 1)
        bias_ref[...] = jnp.abs(r - c).astype(F32) * (-slope)

    q = q_ref[0]
    first_half = lax.broadcasted_iota(I32, (1, ATTN_V_DIM), 1) < ATTN_HEAD_DIM
    zero = jnp.zeros_like(q)
    maps = ((ka_ref, first_half, qpa_ref[...]), (kb_ref, ~first_half, qpb_ref[...]))
    starts = [pl.multiple_of(lax.rem(qi + r, nk) * tq, tq) for r in range(nk)]

    row_max = []
    for m, (k_ref, own, q_pos) in enumerate(maps):
        q_diag = jnp.where(own, q, zero)
        q_left = jnp.where(own, q, q_pos)
        q_right = jnp.where(own, q, -q_pos)
        for r in range(nk):
            kb = k_ref[0, pl.ds(starts[r], tq), :]
            if r == 0:
                s = lax.dot_general(q_diag, kb, _NT, preferred_element_type=F32) + bias_ref[...]
            else:
                q_side = jnp.where(lax.rem(qi + r, nk) < qi, q_left, q_right)
                s = lax.dot_general(q_side, kb, _NT, preferred_element_type=F32)
            s_ref[m, r] = s
            blk_max = jnp.max(s, axis=-1, keepdims=True)
            if r == 0:
                row_max.append(blk_max)
            else:
                row_max[m] = jnp.maximum(row_max[m], blk_max)

    outs = []
    for m in range(2):
        acc = jnp.zeros((tq, 2 * ATTN_V_DIM), F32)
        for r in range(nk):
            p = jnp.exp(s_ref[m, r] - row_max[m]).astype(BF16)
            acc = acc + _dot(p, v_ref[0, pl.ds(starts[r], tq), :])
        outs.append(acc[:, :ATTN_V_DIM] / acc[:, ATTN_V_DIM:ATTN_V_DIM + 1])

    o1, o2 = outs
    lq = lq_ref[...]
    lam = (jnp.exp(jnp.sum(lq[0:1] * lq[1:2], axis=-1, keepdims=True))
           - jnp.exp(jnp.sum(lq[2:3] * lq[3:4], axis=-1, keepdims=True)) + LAMBDA_INIT)
    o = o1 - lam * o2
    o_ref[0] = (_rms(o) * g_ref[...] * (1.0 - LAMBDA_INIT)).astype(BF16)


def _diff_attention(q, k_a, k_b, v_ones, qpos_a, qpos_b, lambda_qk, subln_g):
    bsz, seq, _ = q.shape
    tq = _tile(seq, ATTN_Q_ROWS)
    k_spec = pl.BlockSpec((1, seq, ATTN_V_DIM), lambda h, i, b: (b, 0, h))
    v_spec = pl.BlockSpec((1, seq, 2 * ATTN_V_DIM), lambda h, i, b: (b, 0, h))
    q_spec = pl.BlockSpec((1, tq, ATTN_V_DIM), lambda h, i, b: (b, i, h))
    qpos_spec = pl.BlockSpec((tq, ATTN_V_DIM), lambda h, i, b: (i, h))
    return pl.pallas_call(
        _attn_kernel,
        grid=(ATTN_HEADS, seq // tq, bsz),
        in_specs=[q_spec, k_spec, k_spec, v_spec, qpos_spec, qpos_spec,
                  pl.BlockSpec(lambda_qk.shape, lambda h, i, b: (0, 0)),
                  pl.BlockSpec(subln_g.shape, lambda h, i, b: (0, 0))],
        out_specs=q_spec,
        out_shape=jax.ShapeDtypeStruct((bsz, seq, ATTN_WIDTH), BF16),
        scratch_shapes=[pltpu.VMEM((2, seq // tq, tq, tq), F32), pltpu.VMEM((tq, tq), F32)],
        compiler_params=_params(3),
        name="diff_attention",
    )(q, k_a, k_b, v_ones, qpos_a, qpos_b, lambda_qk, subln_g)


def _merge_kernel(yhy_ref, yat_ref, sg_ref, x_ref, mod_ref, why_ref, wat_ref, wmix_ref, g_ref,
                  wr_ref, br_ref, x1_ref, h2_ref, lg_ref):
    d = x_ref.shape[2]
    sg = sg_ref[0]
    merged = (sg[:, :d].astype(F32) * _dot(yhy_ref[0], why_ref[...])
              + sg[:, d:].astype(F32) * _dot(yat_ref[0], wat_ref[...]))
    x1 = x_ref[0] + mod_ref[0, 2:3, :] * _dot(merged.astype(BF16), wmix_ref[...])
    x1_ref[0] = x1
    h2 = _rms(x1) * g_ref[...]
    h2 = h2 * (1.0 + mod_ref[0, 4:5, :]) + mod_ref[0, 3:4, :]
    h2_ref[0] = h2
    lg_ref[0] = _dot3(wr_ref[...], h2, _NT) + br_ref[...]


def _merge(y_hy, y_at, sg, x, mod3, w_hy, w_at, w_mix, g, w_router_t, b_router):
    bsz, seq, d = x.shape
    tm = _tile(seq, MERGE_ROWS)
    row = lambda width: pl.BlockSpec((1, tm, width), lambda b, i: (b, i, 0))
    full = lambda a: pl.BlockSpec(a.shape, lambda b, i: (0,) * a.ndim)
    return pl.pallas_call(
        _merge_kernel,
        grid=(bsz, seq // tm),
        in_specs=[row(y_hy.shape[2]), row(y_at.shape[2]), row(2 * d), row(d),
                  pl.BlockSpec((1, 6, d), lambda b, i: (b, 0, 0)),
                  full(w_hy), full(w_at), full(w_mix), full(g), full(w_router_t), full(b_router)],
        out_specs=[row(d), row(d), pl.BlockSpec((1, N_EXPERTS, tm), lambda b, i: (b, 0, i))],
        out_shape=[jax.ShapeDtypeStruct((bsz, seq, d), F32), jax.ShapeDtypeStruct((bsz, seq, d), F32),
                   jax.ShapeDtypeStruct((bsz, N_EXPERTS, seq), F32)],
        compiler_params=_params(2),
        name="merge_router",
    )(y_hy, y_at, sg, x, mod3, w_hy, w_at, w_mix, g, w_router_t, b_router)


def _route_kernel(lg_ref, idx_ref, gate_ref, rank_ref, cnt_ref, carry_ref):
    step = pl.program_id(0) * pl.num_programs(1) + pl.program_id(1)

    @pl.when(step == 0)
    def _():
        carry_ref[...] = jnp.zeros_like(carry_ref)

    work = lg_ref[0]
    n_exp, tr = work.shape
    e_iota = lax.broadcasted_iota(I32, (n_exp, tr), 0)
    vals, idxs = [], []
    for _ in range(TOP_K):
        m = jnp.max(work, axis=0, keepdims=True)
        ik = jnp.min(jnp.where(work == m, e_iota, n_exp), axis=0, keepdims=True)
        vals.append(m)
        idxs.append(ik)
        work = jnp.where(e_iota == ik, -jnp.inf, work)
    ex = [jnp.exp(val - vals[0]) for val in vals]
    denom = ex[0] + ex[1] + ex[2] + ex[3]
    sel = jnp.zeros((n_exp, tr), F32)
    for ik in idxs:
        sel = sel + jnp.where(e_iota == ik, 1.0, 0.0)
    upper = jnp.where(lax.broadcasted_iota(I32, (tr, tr), 0) < lax.broadcasted_iota(I32, (tr, tr), 1), 1.0, 0.0)
    rank_all = _dot(sel.astype(BF16), upper.astype(BF16)) + carry_ref[:, 0:1]
    pad = ROUTE_SUBLANES - TOP_K
    ranks = [jnp.sum(jnp.where(e_iota == ik, rank_all, 0.0), axis=0, keepdims=True) for ik in idxs]
    idx_ref[...] = jnp.concatenate(idxs + [jnp.zeros((pad, tr), I32)], axis=0)
    gate_ref[...] = jnp.concatenate([e / denom for e in ex] + [jnp.zeros((pad, tr), F32)], axis=0)
    rank_ref[...] = jnp.concatenate(ranks + [jnp.zeros((pad, tr), F32)], axis=0).astype(I32)
    carry_ref[...] = carry_ref[...] + jnp.sum(sel, axis=1, keepdims=True)
    cnt_ref[...] = carry_ref[...]


def _route(logits_t):
    bsz, n_exp, seq = logits_t.shape
    tr = _tile(seq, ROUTE_TOKENS)
    nt = seq // tr
    tok = pl.BlockSpec((ROUTE_SUBLANES, tr), lambda b, i: (0, b * nt + i))
    shape = lambda dt: jax.ShapeDtypeStruct((ROUTE_SUBLANES, bsz * seq), dt)
    return pl.pallas_call(
        _route_kernel,
        grid=(bsz, nt),
        in_specs=[pl.BlockSpec((1, n_exp, tr), lambda b, i: (b, 0, i))],
        out_specs=[tok, tok, tok, pl.BlockSpec((n_exp, 128), lambda b, i: (0, 0))],
        out_shape=[shape(I32), shape(F32), shape(I32), jax.ShapeDtypeStruct((n_exp, 128), F32)],
        scratch_shapes=[pltpu.VMEM((n_exp, 128), F32)],
        compiler_params=_params(2),
        name="route_topk",
    )(logits_t)


def _dest_kernel(idx_ref, rank_ref, start_ref, dest_ref):
    idx = idx_ref[...]
    n_exp = start_ref.shape[0]
    tr = idx.shape[1]
    e_iota = lax.broadcasted_iota(I32, (n_exp, tr), 0)
    start = start_ref[:, 0:1]
    rows = [jnp.sum(jnp.where(e_iota == idx[k:k + 1, :], start, 0), axis=0, keepdims=True) for k in range(TOP_K)]
    rows.append(jnp.zeros((ROUTE_SUBLANES - TOP_K, tr), I32))
    dest_ref[...] = rank_ref[...] + jnp.concatenate(rows, axis=0)


def _dest_rows(idx_t, rank_t, pad_start):
    n_tok = idx_t.shape[1]
    tr = _tile(n_tok, DEST_TOKENS)
    tok = pl.BlockSpec((ROUTE_SUBLANES, tr), lambda i: (0, i))
    return pl.pallas_call(
        _dest_kernel,
        grid=(n_tok // tr,),
        in_specs=[tok, tok, pl.BlockSpec(pad_start.shape, lambda i: (0, 0))],
        out_specs=tok,
        out_shape=jax.ShapeDtypeStruct(idx_t.shape, I32),
        compiler_params=_params(1),
        name="dest_rows",
    )(idx_t, rank_t, pad_start)


def _relayout_copies(flat_ref, tiled_ref, sem, to_tiles):
    copies = []
    for s in range(ROW_TILE_SUBLANES):
        flat = flat_ref.at[:, pl.ds(s * LANES, LANES)]
        tiled = tiled_ref.at[:, s, :]
        copies.append(pltpu.make_async_copy(flat, tiled, sem) if to_tiles else pltpu.make_async_copy(tiled, flat, sem))
    return copies


def _dispatch_kernel(dest_ref, pend_ref, h_ref, xs_ref, rows_ref, zeros_ref, load_sems, row_sems):
    step = pl.program_id(0)
    last = pl.num_programs(0) - 1
    tt = rows_ref.shape[1]
    bm = zeros_ref.shape[0]
    slot = lax.rem(step, 3)
    slot_next = lax.rem(step + 1, 3)
    slot_prev = lax.rem(step + 2, 3)

    def load(tile, sl):
        return _relayout_copies(h_ref.at[pl.ds(tile * tt, tt)], rows_ref.at[sl], load_sems.at[sl], True)

    @pl.when(step == 0)
    def _():
        for c in load(0, 0):
            c.start()
        zeros_ref[...] = jnp.zeros_like(zeros_ref)

        def zero_copy(start):
            return pltpu.make_async_copy(zeros_ref, xs_ref.at[pl.ds(start, bm)], row_sems.at[0])

        for e in range(N_EXPERTS):
            zero_copy(jnp.maximum(pend_ref[e] - bm, 0)).start()
        for e in range(N_EXPERTS):
            zero_copy(0).wait()

        def zero_tail(j, carry):
            zero_copy(j * bm).start()
            zero_copy(0).wait()
            return carry

        lax.fori_loop(pend_ref[N_EXPERTS - 1] // bm, xs_ref.shape[0] // bm, zero_tail, 0)

    @pl.when(step < last)
    def _():
        for c in load(step + 1, slot_next):
            c.start()

    for c in load(step, slot):
        c.wait()

    def issue(t, carry):
        for k in range(TOP_K):
            pltpu.make_async_copy(rows_ref.at[slot, t], xs_ref.at[dest_ref[t * TOP_K + k]],
                                  row_sems.at[slot]).start(priority=k % 2)
        return carry

    lax.fori_loop(0, tt, issue, 0, unroll=4)

    def drain(sl):
        def body(t, carry):
            for _ in range(TOP_K):
                pltpu.make_async_copy(rows_ref.at[sl, 0], xs_ref.at[0], row_sems.at[sl]).wait()
            return carry

        lax.fori_loop(0, tt, body, 0, unroll=4)

    @pl.when(step > 0)
    def _():
        drain(slot_prev)

    @pl.when(step == last)
    def _():
        drain(slot)


def _dispatch(dest_flat, pad_end, h2, n_rows):
    n_tok, d = h2.shape
    assert d == ROW_TILE_SUBLANES * LANES
    tt = _tile(n_tok, DISPATCH_TOKENS)
    tile = (ROW_TILE_SUBLANES, LANES)
    return pl.pallas_call(
        _dispatch_kernel,
        grid=(n_tok // tt,),
        in_specs=[pl.BlockSpec((tt * TOP_K,), lambda i: (i,), memory_space=pltpu.SMEM),
                  pl.BlockSpec(memory_space=pltpu.SMEM),
                  pl.BlockSpec(memory_space=pl.ANY)],
        out_specs=pl.BlockSpec(memory_space=pl.ANY),
        out_shape=jax.ShapeDtypeStruct((n_rows,) + tile, F32),
        scratch_shapes=[pltpu.VMEM((3, tt) + tile, F32), pltpu.VMEM((EXPERT_ROWS,) + tile, F32),
                        pltpu.SemaphoreType.DMA((3,)), pltpu.SemaphoreType.DMA((3,))],
        compiler_params=_params(1),
        name="moe_dispatch",
    )(dest_flat, pad_end, h2)


def _expert_kernel(be_ref, nu_ref, xs_ref, wgu_ref, bgu_ref, wd_ref, bd_ref, ys_ref,
                   x_buf, y_buf, wgu_bf16, wd_bf16, in_sems, out_sems):
    step = pl.program_id(0)
    n_used = nu_ref[0]
    bm = x_buf.shape[1]
    de = wd_ref.shape[1]
    slot = step % 2

    def load(block, sl):
        return _relayout_copies(x_buf.at[sl], xs_ref.at[pl.ds(block * bm, bm)], in_sems.at[sl], False)

    def store(block, sl):
        return _relayout_copies(y_buf.at[sl], ys_ref.at[pl.ds(block * bm, bm)], out_sems.at[sl], True)

    @pl.when(step < n_used)
    def _():
        @pl.when(step == 0)
        def _():
            for c in load(0, 0):
                c.start()

        @pl.when(step + 1 < n_used)
        def _():
            for c in load(step + 1, 1 - slot):
                c.start()

        @pl.when((step == 0) | (be_ref[step] != be_ref[jnp.maximum(step - 1, 0)]))
        def _():
            wgu_bf16[...] = wgu_ref[0].astype(BF16)
            wd_bf16[...] = wd_ref[0].astype(BF16)

        for c in load(step, slot):
            c.wait()
        gu = _dot(x_buf[slot].astype(BF16), wgu_bf16[...]) + bgu_ref[0]
        gate = jnp.minimum(gu[:, :de], SWIGLU_LIMIT)
        up = jnp.clip(gu[:, de:], -SWIGLU_LIMIT, SWIGLU_LIMIT)
        glu = gate * jax.nn.sigmoid(SWIGLU_ALPHA * gate)
        y = _dot(((up + 1.0) * glu).astype(BF16), wd_bf16[...]) + bd_ref[0]

        @pl.when(step >= 2)
        def _():
            for c in store(step - 2, slot):
                c.wait()

        y_buf[slot] = y
        for c in store(step, slot):
            c.start()

        @pl.when(step == n_used - 1)
        def _():
            @pl.when(step >= 1)
            def _():
                for c in store(step - 1, 1 - slot):
                    c.wait()

            for c in store(step, slot):
                c.wait()

    @pl.when(step >= n_used)
    def _():
        y_buf[slot] = jnp.zeros(y_buf.shape[1:], F32)
        for c in store(step, slot):
            c.start()
        for c in store(step, slot):
            c.wait()


def _experts(block_expert, n_used, xs, w_gu, b_gu, w_d, b_d):
    n_rows = xs.shape[0]
    n_blocks = n_rows // EXPERT_ROWS
    _, d, de2 = w_gu.shape
    de = w_d.shape[1]
    per_expert = lambda *dims: pl.BlockSpec((1,) + dims, lambda i, be, nu: (be[i], 0, 0))
    return pl.pallas_call(
        _expert_kernel,
        grid_spec=pltpu.PrefetchScalarGridSpec(
            num_scalar_prefetch=2,
            grid=(n_blocks,),
            in_specs=[pl.BlockSpec(memory_space=pl.ANY), per_expert(d, de2), per_expert(1, de2),
                      per_expert(de, d), per_expert(1, d)],
            out_specs=pl.BlockSpec(memory_space=pl.ANY),
            scratch_shapes=[pltpu.VMEM((2, EXPERT_ROWS, d), F32), pltpu.VMEM((2, EXPERT_ROWS, d), F32),
                            pltpu.VMEM((d, de2), BF16), pltpu.VMEM((de, d), BF16),
                            pltpu.SemaphoreType.DMA((2,)), pltpu.SemaphoreType.DMA((2,))],
        ),
        out_shape=jax.ShapeDtypeStruct(xs.shape, F32),
        compiler_params=_params(1),
        name="moe_experts",
    )(block_expert, n_used, xs, w_gu, b_gu, w_d, b_d)


def _combine_kernel(idx_ref, next_ref, gate_ref, x1_ref, mod_ref, g_ref, ys_ref, o_ref, buf, sems):
    step = pl.program_id(0)
    tt, d = x1_ref.shape
    groups = tt // ROW_TILE_SUBLANES
    slot = step % 2

    def issue_gathers(rows_ref, sl):
        def body(i, carry):
            for j in range(ROW_TILE_SUBLANES):
                for k in range(TOP_K):
                    row = rows_ref[(i * ROW_TILE_SUBLANES + j) * TOP_K + k]
                    pltpu.make_async_copy(ys_ref.at[row], buf.at[sl, k, i, :, j, :], sems.at[sl]).start(priority=k % 2)
            return carry

        lax.fori_loop(0, groups, body, 0)

    @pl.when(step == 0)
    def _():
        issue_gathers(idx_ref, 0)

    @pl.when(step + 1 < pl.num_programs(0))
    def _():
        issue_gathers(next_ref, 1 - slot)

    def drain(i, carry):
        for _ in range(ROW_TILE_SUBLANES * TOP_K):
            pltpu.make_async_copy(ys_ref.at[0], buf.at[slot, 0, 0, :, 0, :], sems.at[slot]).wait()
        return carry

    lax.fori_loop(0, groups, drain, 0)

    eye = lax.broadcasted_iota(I32, (tt, tt), 0) == lax.broadcasted_iota(I32, (tt, tt), 1)
    gates = gate_ref[...]
    cols = [jnp.sum(jnp.where(eye, gates[k:k + 1, :], 0.0), axis=1, keepdims=True) for k in range(TOP_K)]
    chunks = []
    sq = jnp.zeros((tt, LANES), F32)
    for c in range(d // LANES):
        lanes = slice(c * LANES, (c + 1) * LANES)
        moe = cols[0] * buf[slot, 0, :, c].reshape(tt, LANES)
        for k in range(1, TOP_K):
            moe = moe + cols[k] * buf[slot, k, :, c].reshape(tt, LANES)
        x2 = x1_ref[:, lanes] + mod_ref[0, 5:6, lanes] * moe
        sq = sq + x2 * x2
        chunks.append(x2)
    scale = lax.rsqrt(jnp.sum(sq, axis=-1, keepdims=True) / d + EPS)
    for c, x2 in enumerate(chunks):
        lanes = slice(c * LANES, (c + 1) * LANES)
        o_ref[:, lanes] = x2 * scale * g_ref[:, lanes]


def _combine(dest_flat, gate_t, x1, mod3, g, ys, seq):
    n_tok, d = x1.shape
    tt = _tile(seq, COMBINE_TOKENS)
    per_seq = seq // tt
    n_steps = n_tok // tt
    idx_spec = lambda ahead: pl.BlockSpec((tt * TOP_K,), lambda i: (jnp.minimum(i + ahead, n_steps - 1),),
                                          memory_space=pltpu.SMEM)
    return pl.pallas_call(
        _combine_kernel,
        grid=(n_steps,),
        in_specs=[idx_spec(0), idx_spec(1),
                  pl.BlockSpec((ROUTE_SUBLANES, tt), lambda i: (0, i)),
                  pl.BlockSpec((tt, d), lambda i: (i, 0)),
                  pl.BlockSpec((1, 6, d), lambda i: (i // per_seq, 0, 0)),
                  pl.BlockSpec((1, d), lambda i: (0, 0)),
                  pl.BlockSpec(memory_space=pl.ANY)],
        out_specs=pl.BlockSpec((tt, d), lambda i: (i, 0)),
        out_shape=jax.ShapeDtypeStruct((n_tok, d), F32),
        scratch_shapes=[pltpu.VMEM((2, TOP_K, tt // ROW_TILE_SUBLANES, d // LANES, ROW_TILE_SUBLANES, LANES), F32),
                        pltpu.SemaphoreType.DMA((2,))],
        compiler_params=_params(1),
        name="moe_combine",
    )(dest_flat, dest_flat, gate_t, x1, mod3, g, ys)


def kernel(x, c, w_ada, b_ada, norm_mix_g, w_in, b_in, hy_conv_w, hy_conv_b, filt_w1, filt_b1, filt_w2, filt_b2, filt_w3, filt_b3, filt_w4, filt_freq, hy_d_skip, lambda_qk, attn_subln_g, w_hy_out, w_attn_out, w_mix_out, norm_ffn_g, w_router, b_router, w_gate_up, b_gate_up, w_down, b_down, final_norm_g):
    bsz, seq, d = x.shape
    depth = w_ada.shape[0]
    n_tok = bsz * seq
    row2 = lambda a: a.reshape(1, -1)

    cos_t, sin_t = _dft_tables(seq)
    cos_b = jnp.asarray(cos_t).astype(BF16)
    sin_b = jnp.asarray(sin_t).astype(BF16)
    z = _filter_features(seq)
    min_decay = math.log(DECAY_TARGET) / SLOW_DECAY_PCT
    max_decay = math.log(DECAY_TARGET) / FAST_DECAY_PCT
    absdelta = jnp.abs(jnp.linspace(min_decay, max_decay, HYENA_WIDTH, dtype=F32))[None, :]
    kpos_a, kpos_b, qpos_a, qpos_b = (t.astype(BF16) for t in _alibi_tables(seq))

    n_blocks = -(-n_tok * TOP_K // EXPERT_ROWS) + N_EXPERTS
    n_rows = n_blocks * EXPERT_ROWS

    assert depth == 1, "LAMBDA_INIT is the first layer's"
    for l in range(depth):
        mod3 = _adaln_mod(c, w_ada[l], row2(b_ada[l])).reshape(bsz, 6, d)

        u_hy, q, k_a, k_b, v_ones, sg = _in_proj(x, mod3, row2(norm_mix_g[l]), w_in[l].astype(BF16), row2(b_in[l]),
                                                 kpos_a, kpos_b)

        w1p = jnp.pad(filt_w1[l], ((0, 128 - filt_w1.shape[1]), (0, 0)))
        kr, kq = _hyena_filters(z, w1p, row2(filt_b1[l]), filt_w2[l], row2(filt_b2[l]), filt_w3[l],
                                row2(filt_b3[l]), filt_w4[l], row2(filt_freq[l]), absdelta, cos_b, sin_b)
        y_hy = _hyena(u_hy, hy_conv_w[l], row2(hy_conv_b[l]), row2(hy_d_skip[l]), cos_b, sin_b, kr, kq)

        y_at = _diff_attention(q, k_a, k_b, v_ones, qpos_a, qpos_b, lambda_qk[l], row2(attn_subln_g[l]))

        x1, h2, logits_t = _merge(
            y_hy, y_at, sg, x, mod3, w_hy_out[l].astype(BF16), w_attn_out[l].astype(BF16),
            w_mix_out[l].astype(BF16), row2(norm_ffn_g[l]), w_router[l].T, b_router[l].reshape(-1, 1))

        idx_t, gate_t, rank_t, counts = _route(logits_t)
        counts = counts[:, 0].astype(I32)
        padded = ((counts + EXPERT_ROWS - 1) // EXPERT_ROWS) * EXPERT_ROWS
        pad_end = jnp.cumsum(padded)
        pad_start = pad_end - padded
        dest_t = _dest_rows(idx_t, rank_t, jnp.broadcast_to(pad_start[:, None], (N_EXPERTS, 128)))
        dest_flat = dest_t[:TOP_K].T.reshape(-1)
        block_start = jnp.arange(n_blocks, dtype=I32) * EXPERT_ROWS
        block_expert = jnp.sum((block_start[:, None] >= pad_end[None, :]).astype(I32), axis=1)
        block_expert = jnp.minimum(block_expert, N_EXPERTS - 1)
        n_used = (pad_end[-1:] // EXPERT_ROWS).astype(I32)

        xs = _dispatch(dest_flat, pad_end.astype(I32), h2.reshape(n_tok, d), n_rows)
        ys = _experts(block_expert, n_used, xs, w_gate_up[l], b_gate_up[l][:, None, :],
                      w_down[l], b_down[l][:, None, :])
        x = _combine(dest_flat, gate_t, x1.reshape(n_tok, d), mod3, row2(final_norm_g), ys, seq).reshape(bsz, seq, d)
    return x
```

```python
import functools
import math

import numpy as np
import jax
import jax.numpy as jnp
from jax import lax
from jax.experimental import pallas as pl
from jax.experimental.pallas import tpu as pltpu

F32 = jnp.float32
BF16 = jnp.bfloat16
I32 = jnp.int32

EPS = 1e-5
HYENA_WIDTH = 512
FILTER_BANDS = 8
DECAY_TARGET = 1e-2
FAST_DECAY_PCT = 0.3
SLOW_DECAY_PCT = 1.5
ATTN_HEADS = 4
ATTN_HEAD_DIM = 64
ATTN_V_DIM = 2 * ATTN_HEAD_DIM
ATTN_WIDTH = ATTN_HEADS * ATTN_V_DIM
N_EXPERTS = 32
TOP_K = 4
SWIGLU_LIMIT = 7.0
SWIGLU_ALPHA = 1.702
LAMBDA_INIT = 0.8 - 0.6 * math.exp(-0.3 * 0)
ALIBI_SPLIT_LOG2 = 6

IN_PROJ_ROWS = 512
ATTN_Q_ROWS = 512
HYENA_BLOCK = 512
MERGE_ROWS = 512
ROUTE_TOKENS = 512
DEST_TOKENS = 2048
DISPATCH_TOKENS = 256
EXPERT_ROWS = 512
COMBINE_TOKENS = 256
ROUTE_SUBLANES = 8
ROW_TILE_SUBLANES = 8
LANES = 128

VMEM_LIMIT_BYTES = 56 * 1024 * 1024


def _tile(n, t):
    t = min(n, t)
    assert n % t == 0, (n, t)
    return t


def _params(n_axes):
    return pltpu.CompilerParams(
        dimension_semantics=("arbitrary",) * n_axes, vmem_limit_bytes=VMEM_LIMIT_BYTES
    )


def _split_bf16(a):
    hi = a.astype(BF16)
    lo = (a - hi.astype(F32)).astype(BF16)
    return hi, lo


_NN = (((1,), (0,)), ((), ()))
_NT = (((1,), (1,)), ((), ()))


def _dot3(a, b, dims=_NN):
    ah, al = _split_bf16(a)
    bh, bl = _split_bf16(b)
    d = lambda x, y: lax.dot_general(x, y, dims, preferred_element_type=F32)
    return d(ah, bh) + d(ah, bl) + d(al, bh)


def _dot(a, b):
    return jnp.dot(a, b, preferred_element_type=F32)


def _rms(x):
    return x * lax.rsqrt(jnp.mean(x * x, axis=-1, keepdims=True) + EPS)


def _mod_kernel(c_ref, w_ref, b_ref, o_ref):
    c = c_ref[...]
    o_ref[...] = _dot3(c * jax.nn.sigmoid(c), w_ref[...]) + b_ref[...]


def _adaln_mod(c, w, b):
    bsz, d = c.shape
    n = w.shape[1]
    tn = _tile(n, 1536)
    return pl.pallas_call(
        _mod_kernel,
        grid=(n // tn,),
        in_specs=[
            pl.BlockSpec((bsz, d), lambda j: (0, 0)),
            pl.BlockSpec((d, tn), lambda j: (0, j)),
            pl.BlockSpec((1, tn), lambda j: (0, j)),
        ],
        out_specs=pl.BlockSpec((bsz, tn), lambda j: (0, j)),
        out_shape=jax.ShapeDtypeStruct((bsz, n), F32),
        compiler_params=_params(1),
        name="adaln_mod",
    )(c, w, b)


def _alibi_tables(seq):
    slopes = 2.0 ** (-8.0 * jnp.arange(1, ATTN_HEADS + 1, dtype=F32) / ATTN_HEADS)
    pos = jnp.arange(seq, dtype=I32)
    hi = (jnp.right_shift(pos, ALIBI_SPLIT_LOG2) << ALIBI_SPLIT_LOG2).astype(F32)[:, None] * slopes[None, :]
    lo = (pos & ((1 << ALIBI_SPLIT_LOG2) - 1)).astype(F32)[:, None] * slopes[None, :]
    one = jnp.ones_like(hi)
    key4 = jnp.stack([hi, lo, one, one], axis=-1)
    qry4 = jnp.stack([one, one, -hi, -lo], axis=-1)

    def place(vals, first_lane):
        tab = jnp.zeros((seq, ATTN_HEADS, ATTN_V_DIM), F32).at[:, :, first_lane:first_lane + 4].set(vals)
        return tab.reshape(seq, ATTN_WIDTH)

    half = ATTN_HEAD_DIM
    return place(key4, half), place(key4, 0), place(qry4, half), place(qry4, 0)


def _in_proj_kernel(x_ref, mod_ref, g_ref, w_ref, b_ref, kpa_ref, kpb_ref,
                    uhy_ref, q_ref, ka_ref, kb_ref, v_ref, sg_ref, *, cols):
    x = x_ref[0]
    h = _rms(x) * g_ref[...]
    h = h * (1.0 + mod_ref[0, 1:2, :]) + mod_ref[0, 0:1, :]
    hb = h.astype(BF16)

    def proj(lo, hi):
        return _dot(hb, w_ref[:, lo:hi]) + b_ref[:, lo:hi]

    c0, c1, c2, c3, c4 = cols
    uhy_ref[0] = proj(0, c0).astype(BF16)
    q_ref[0] = (proj(c0, c1) * (1.0 / math.sqrt(ATTN_HEAD_DIM))).astype(BF16)
    k = proj(c1, c2).astype(BF16)
    first_half = (lax.broadcasted_iota(I32, (1, ATTN_WIDTH), 1) % ATTN_V_DIM) < ATTN_HEAD_DIM
    ka_ref[0] = jnp.where(first_half, k, kpa_ref[...])
    kb_ref[0] = jnp.where(first_half, kpb_ref[...], k)
    v = proj(c2, c3).astype(BF16)
    ones_col = jnp.where(lax.broadcasted_iota(I32, (v.shape[0], ATTN_V_DIM), 1) == 0, 1.0, 0.0).astype(BF16)
    pieces = []
    for hd in range(ATTN_HEADS):
        pieces += [v[:, hd * ATTN_V_DIM:(hd + 1) * ATTN_V_DIM], ones_col]
    v_ref[0] = jnp.concatenate(pieces, axis=1)
    sg_ref[0] = jax.nn.sigmoid(proj(c3, c4)).astype(BF16)


def _in_proj(x, mod3, g, w_bf16, b, kpos_a, kpos_b):
    bsz, seq, d = x.shape
    n = w_bf16.shape[1]
    hw3 = 3 * HYENA_WIDTH
    cols = (hw3, hw3 + ATTN_WIDTH, hw3 + 2 * ATTN_WIDTH, hw3 + 3 * ATTN_WIDTH, n)
    tm = _tile(seq, IN_PROJ_ROWS)
    row = lambda width: pl.BlockSpec((1, tm, width), lambda bi, i: (bi, i, 0))
    out = lambda width: jax.ShapeDtypeStruct((bsz, seq, width), BF16)
    pos = pl.BlockSpec((tm, ATTN_WIDTH), lambda bi, i: (i, 0))
    widths = (hw3, ATTN_WIDTH, ATTN_WIDTH, ATTN_WIDTH, 2 * ATTN_WIDTH, 2 * d)
    return pl.pallas_call(
        functools.partial(_in_proj_kernel, cols=cols),
        grid=(bsz, seq // tm),
        in_specs=[
            row(d),
            pl.BlockSpec((1, 6, d), lambda bi, i: (bi, 0, 0)),
            pl.BlockSpec((1, d), lambda bi, i: (0, 0)),
            pl.BlockSpec((d, n), lambda bi, i: (0, 0)),
            pl.BlockSpec((1, n), lambda bi, i: (0, 0)),
            pos, pos,
        ],
        out_specs=[row(w) for w in widths],
        out_shape=[out(w) for w in widths],
        compiler_params=_params(2),
        name="in_proj",
    )(x, mod3, g, w_bf16, b, kpos_a, kpos_b)


@functools.lru_cache(maxsize=None)
def _dft_tables(p):
    n_fft = 2 * p
    f = np.arange(p, dtype=np.int64)[:, None]
    ang = lambda n: 2.0 * np.pi * ((f * n[None, :]) % n_fft).astype(np.float64) / n_fft
    n_blk = np.arange(p, dtype=np.int64)
    n_seg = np.arange(-p, p, dtype=np.int64)
    keep = (n_seg != -p)[None, :]
    f32 = lambda a: a.astype(np.float32)
    return (f32(np.cos(ang(n_blk))), f32(np.sin(ang(n_blk))),
            f32(np.cos(ang(n_seg)) * keep), f32(-np.sin(ang(n_seg)) * keep))


def _filter_features(seq):
    t = jnp.linspace(0.0, 1.0, seq, dtype=F32)[:, None]
    w = 2.0 * math.pi * jnp.arange(seq, dtype=F32)[:, None] / seq
    fr = jnp.linspace(1e-4, FILTER_BANDS - 1, FILTER_BANDS, dtype=F32)[None, :]
    z = jnp.concatenate([t, jnp.cos(fr * w), -jnp.sin(fr * w)], axis=-1)
    z = jnp.pad(z, ((0, 0), (0, 128 - z.shape[1])))
    z_rev = jnp.concatenate([z[:1], z[:0:-1]], axis=0)
    return z, z_rev


def _alternating(rows):
    return jnp.where((rows & 1) == 0, 1.0, -1.0).astype(F32)


def _filter_kernel(z_ref, zrev_ref, w1_ref, b1_ref, w2_ref, b2_ref, w3_ref, b3_ref, w4_ref, fq_ref, ad_ref,
                   segcos_ref, segsin_ref, kr_ref, kq_ref, taps_ref):
    seq = z_ref.shape[0]
    width = ad_ref.shape[1]
    p = segcos_ref.shape[0]
    fq = fq_ref[...]

    def taps(feat, lo):
        h = jnp.sin(fq * (_dot3(feat, w1_ref[...]) + b1_ref[...]))
        h = jnp.sin(fq * (_dot3(h, w2_ref[...]) + b2_ref[...]))
        h = jnp.sin(fq * (_dot3(h, w3_ref[...]) + b3_ref[...]))
        return _dot3(h, w4_ref[:, lo:lo + width]) * jnp.exp(-feat[:, 0:1] * ad_ref[...])

    lag_is_minus_l = lax.broadcasted_iota(I32, (seq, 1), 0) == 0
    taps_ref[0:seq, :] = jnp.where(lag_is_minus_l, 0.0, taps(zrev_ref[...], width))
    taps_ref[seq:2 * seq, :] = taps(z_ref[...], 0)

    rows = lax.broadcasted_iota(I32, (p, 1), 0)
    first = rows == 0
    scale = jnp.where(first, 0.5 / p, 1.0 / p)
    lag = lax.broadcasted_iota(I32, (2 * p, 1), 0)
    alt = jnp.where(lag == 0, 0.0, _alternating(lag))
    n_seg = kr_ref.shape[0]
    for s in range(n_seg):
        d = s - n_seg // 2
        seg = taps_ref[(d - 1) * p + seq:(d + 1) * p + seq, :]
        seg_b = seg.astype(BF16)
        kr = _dot(segcos_ref[...], seg_b)
        kq = jnp.where(first, jnp.sum(alt * seg, axis=0, keepdims=True), _dot(segsin_ref[...], seg_b))
        kr_ref[s] = kr * scale
        kq_ref[s] = kq * scale


def _hyena_filters(z, z_rev, w1, b1, w2, b2, w3, b3, w4, fq, absdelta, seg_cos, seg_nsin):
    seq = z.shape[0]
    width = absdelta.shape[1]
    p = seg_cos.shape[0]
    n_seg = 2 * (seq // p) - 1
    args = (z, z_rev, w1, b1, w2, b2, w3, b3, w4, fq, absdelta, seg_cos, seg_nsin)
    full = lambda a: pl.BlockSpec(a.shape, lambda i: (0,) * a.ndim)
    return pl.pallas_call(
        _filter_kernel,
        grid=(1,),
        in_specs=[full(a) for a in args],
        out_specs=[pl.BlockSpec((n_seg, p, width), lambda i: (0, 0, 0))] * 2,
        out_shape=[jax.ShapeDtypeStruct((n_seg, p, width), F32)] * 2,
        scratch_shapes=[pltpu.VMEM((2 * seq, width), F32)],
        compiler_params=_params(1),
        name="hyena_filters",
    )(*args)


def _hyena_kernel(x0_ref, x1_ref, v_ref, w0_ref, w1_ref, wv_ref, b0_ref, b1_ref, bv_ref, dskip_ref,
                  cos_ref, sin_ref, kr_ref, kq_ref, o_ref):
    seq = x0_ref.shape[1]
    rows = lax.broadcasted_iota(I32, (seq, 1), 0)
    first = rows == 0
    last = rows == seq - 1
    alt = _alternating(rows)

    def short_conv(u_ref, w_ref, b_ref):
        u = u_ref[0].astype(F32)
        prev = jnp.where(first, 0.0, pltpu.roll(u, 1, 0))
        nxt = jnp.where(last, 0.0, pltpu.roll(u, seq - 1, 0))
        w = w_ref[...]
        return prev * w[0:1] + u * w[1:2] + nxt * w[2:3] + b_ref[...]

    x0 = short_conv(x0_ref, w0_ref, b0_ref)
    x1 = short_conv(x1_ref, w1_ref, b1_ref)
    v = short_conv(v_ref, wv_ref, bv_ref) * x1
    vb = v.astype(BF16)
    cos_m = cos_ref[...]
    sin_m = sin_ref[...]
    p = cos_m.shape[0]
    nb = seq // p
    first_p = first[:p]
    alt_p = alt[:p]
    vr, vq = [], []
    for j in range(nb):
        blk = slice(j * p, (j + 1) * p)
        vr.append(_dot(cos_m, vb[blk]))
        vq.append(jnp.where(first_p, jnp.sum(alt_p * v[blk], axis=0, keepdims=True), _dot(sin_m, vb[blk])))
    y_blocks = []
    for i in range(nb):
        yr = yq = dc = nyq = None
        for j in range(nb):
            kr = kr_ref[i - j + nb - 1]
            kq = kq_ref[i - j + nb - 1]
            terms = (vr[j] * kr + vq[j] * kq, vq[j] * kr - vr[j] * kq, vr[j][0:1] * kr[0:1], vq[j][0:1] * kq[0:1])
            yr, yq, dc, nyq = terms if j == 0 else (yr + terms[0], yq + terms[1], dc + terms[2], nyq + terms[3])
        yr = jnp.where(first_p, dc, yr)
        yq = jnp.where(first_p, nyq, yq)
        y_blocks.append(_dot(cos_m, yr.astype(BF16)) + _dot(sin_m, yq.astype(BF16)) + alt_p * nyq)
    y = jnp.concatenate(y_blocks, axis=0)
    o_ref[0] = ((y + v * dskip_ref[...]) * x0).astype(BF16)


def _hyena(u_hy, conv_w, conv_b, d_skip, cos_b, sin_b, kr, kq):
    bsz, seq, _ = u_hy.shape
    width = d_skip.shape[1]
    p = cos_b.shape[0]
    cw = _tile(width, 256)
    nc = width // cw
    u_spec = lambda part: pl.BlockSpec((1, seq, cw), lambda c, b: (b, 0, part * nc + c))
    w_spec = lambda part: pl.BlockSpec((3, cw), lambda c, b: (0, part * nc + c))
    b_spec = lambda part: pl.BlockSpec((1, cw), lambda c, b: (0, part * nc + c))
    full = pl.BlockSpec((p, p), lambda c, b: (0, 0))
    k_spec = pl.BlockSpec((kr.shape[0], p, cw), lambda c, b: (0, 0, c))
    return pl.pallas_call(
        _hyena_kernel,
        grid=(nc, bsz),
        in_specs=[u_spec(0), u_spec(1), u_spec(2), w_spec(0), w_spec(1), w_spec(2),
                  b_spec(0), b_spec(1), b_spec(2), pl.BlockSpec((1, cw), lambda c, b: (0, c)),
                  full, full, k_spec, k_spec],
        out_specs=pl.BlockSpec((1, seq, cw), lambda c, b: (b, 0, c)),
        out_shape=jax.ShapeDtypeStruct((bsz, seq, width), BF16),
        compiler_params=_params(2),
        name="hyena",
    )(u_hy, u_hy, u_hy, conv_w, conv_w, conv_w, conv_b, conv_b, conv_b, d_skip, cos_b, sin_b, kr, kq)


def _attn_kernel(q_ref, ka_ref, kb_ref, v_ref, qpa_ref, qpb_ref, lq_ref, g_ref, o_ref, s_ref, bias_ref):
    head = pl.program_id(0)
    qi = pl.program_id(1)
    tq = q_ref.shape[1]
    seq = ka_ref.shape[1]
    nk = seq // tq

    @pl.when(pl.program_id(2) == 0)
    def _():
        slope = jnp.where(head == 0, 0.25, jnp.where(head == 1, 0.0625, jnp.where(head == 2, 0.015625, 0.00390625)))
        r = lax.broadcasted_iota(I32, (tq, tq), 0)
        c = lax.broadcasted_iota(I32, (tq, tq), 1)
        bias_ref[...] = jnp.abs(r - c).astype(F32) * (-slope)

    q = q_ref[0]
    first_half = lax.broadcasted_iota(I32, (1, ATTN_V_DIM), 1) < ATTN_HEAD_DIM
    zero = jnp.zeros_like(q)
    maps = ((ka_ref, first_half, qpa_ref[...]), (kb_ref, ~first_half, qpb_ref[...]))
    starts = [pl.multiple_of(lax.rem(qi + r, nk) * tq, tq) for r in range(nk)]

    row_max = []
    for m, (k_ref, own, q_pos) in enumerate(maps):
        q_diag = jnp.where(own, q, zero)
        q_left = jnp.where(own, q, q_pos)
        q_right = jnp.where(own, q, -q_pos)
        for r in range(nk):
            kb = k_ref[0, pl.ds(starts[r], tq), :]
            if r == 0:
                s = lax.dot_general(q_diag, kb, _NT, preferred_element_type=F32) + bias_ref[...]
            else:
                q_side = jnp.where(lax.rem(qi + r, nk) < qi, q_left, q_right)
                s = lax.dot_general(q_side, kb, _NT, preferred_element_type=F32)
            s_ref[m, r] = s
            blk_max = jnp.max(s, axis=-1, keepdims=True)
            if r == 0:
                row_max.append(blk_max)
            else:
                row_max[m] = jnp.maximum(row_max[m], blk_max)

    outs = []
    for m in range(2):
        acc = jnp.zeros((tq, 2 * ATTN_V_DIM), F32)
        for r in range(nk):
            p = jnp.exp(s_ref[m, r] - row_max[m]).astype(BF16)
            acc = acc + _dot(p, v_ref[0, pl.ds(starts[r], tq), :])
        outs.append(acc[:, :ATTN_V_DIM] / acc[:, ATTN_V_DIM:ATTN_V_DIM + 1])

    o1, o2 = outs
    lq = lq_ref[...]
    lam = (jnp.exp(jnp.sum(lq[0:1] * lq[1:2], axis=-1, keepdims=True))
           - jnp.exp(jnp.sum(lq[2:3] * lq[3:4], axis=-1, keepdims=True)) + LAMBDA_INIT)
    o = o1 - lam * o2
    o_ref[0] = (_rms(o) * g_ref[...] * (1.0 - LAMBDA_INIT)).astype(BF16)


def _diff_attention(q, k_a, k_b, v_ones, qpos_a, qpos_b, lambda_qk, subln_g):
    bsz, seq, _ = q.shape
    tq = _tile(seq, ATTN_Q_ROWS)
    k_spec = pl.BlockSpec((1, seq, ATTN_V_DIM), lambda h, i, b: (b, 0, h))
    v_spec = pl.BlockSpec((1, seq, 2 * ATTN_V_DIM), lambda h, i, b: (b, 0, h))
    q_spec = pl.BlockSpec((1, tq, ATTN_V_DIM), lambda h, i, b: (b, i, h))
    qpos_spec = pl.BlockSpec((tq, ATTN_V_DIM), lambda h, i, b: (i, h))
    return pl.pallas_call(
        _attn_kernel,
        grid=(ATTN_HEADS, seq // tq, bsz),
        in_specs=[q_spec, k_spec, k_spec, v_spec, qpos_spec, qpos_spec,
                  pl.BlockSpec(lambda_qk.shape, lambda h, i, b: (0, 0)),
                  pl.BlockSpec(subln_g.shape, lambda h, i, b: (0, 0))],
        out_specs=q_spec,
        out_shape=jax.ShapeDtypeStruct((bsz, seq, ATTN_WIDTH), BF16),
        scratch_shapes=[pltpu.VMEM((2, seq // tq, tq, tq), F32), pltpu.VMEM((tq, tq), F32)],
        compiler_params=_params(3),
        name="diff_attention",
    )(q, k_a, k_b, v_ones, qpos_a, qpos_b, lambda_qk, subln_g)


def _merge_kernel(yhy_ref, yat_ref, sg_ref, x_ref, mod_ref, why_ref, wat_ref, wmix_ref, g_ref,
                  wr_ref, br_ref, x1_ref, h2_ref, lg_ref):
    d = x_ref.shape[2]
    n_exp = lg_ref.shape[1]
    sg = sg_ref[0]
    merged = (sg[:, :d].astype(F32) * _dot(yhy_ref[0], why_ref[...])
              + sg[:, d:].astype(F32) * _dot(yat_ref[0], wat_ref[...]))
    x1 = x_ref[0] + mod_ref[0, 2:3, :] * _dot(merged.astype(BF16), wmix_ref[...])
    x1_ref[0] = x1
    h2 = _rms(x1) * g_ref[...]
    h2 = h2 * (1.0 + mod_ref[0, 4:5, :]) + mod_ref[0, 3:4, :]
    h2_ref[0] = h2
    logits = _dot3(h2, wr_ref[...])
    lg_ref[0] = logits.T[:n_exp, :] + br_ref[...]


def _merge(y_hy, y_at, sg, x, mod3, w_hy, w_at, w_mix, g, w_router, b_router):
    bsz, seq, d = x.shape
    tm = _tile(seq, MERGE_ROWS)
    row = lambda width: pl.BlockSpec((1, tm, width), lambda b, i: (b, i, 0))
    full = lambda a: pl.BlockSpec(a.shape, lambda b, i: (0,) * a.ndim)
    return pl.pallas_call(
        _merge_kernel,
        grid=(bsz, seq // tm),
        in_specs=[row(y_hy.shape[2]), row(y_at.shape[2]), row(2 * d), row(d),
                  pl.BlockSpec((1, 6, d), lambda b, i: (b, 0, 0)),
                  full(w_hy), full(w_at), full(w_mix), full(g), full(w_router), full(b_router)],
        out_specs=[row(d), row(d), pl.BlockSpec((1, N_EXPERTS, tm), lambda b, i: (b, 0, i))],
        out_shape=[jax.ShapeDtypeStruct((bsz, seq, d), F32), jax.ShapeDtypeStruct((bsz, seq, d), F32),
                   jax.ShapeDtypeStruct((bsz, N_EXPERTS, seq), F32)],
        compiler_params=_params(2),
        name="merge_router",
    )(y_hy, y_at, sg, x, mod3, w_hy, w_at, w_mix, g, w_router, b_router)


def _route_kernel(lg_ref, idx_ref, gate_ref, rank_ref, cnt_ref, carry_ref):
    step = pl.program_id(0) * pl.num_programs(1) + pl.program_id(1)

    @pl.when(step == 0)
    def _():
        carry_ref[...] = jnp.zeros_like(carry_ref)

    work = lg_ref[0]
    n_exp, tr = work.shape
    e_iota = lax.broadcasted_iota(I32, (n_exp, tr), 0)
    vals, idxs = [], []
    for _ in range(TOP_K):
        m = jnp.max(work, axis=0, keepdims=True)
        ik = jnp.min(jnp.where(work == m, e_iota, n_exp), axis=0, keepdims=True)
        vals.append(m)
        idxs.append(ik)
        work = jnp.where(e_iota == ik, -jnp.inf, work)
    ex = [jnp.exp(val - vals[0]) for val in vals]
    denom = ex[0] + ex[1] + ex[2] + ex[3]
    sel = jnp.zeros((n_exp, tr), F32)
    for ik in idxs:
        sel = sel + jnp.where(e_iota == ik, 1.0, 0.0)
    upper = jnp.where(lax.broadcasted_iota(I32, (tr, tr), 0) < lax.broadcasted_iota(I32, (tr, tr), 1), 1.0, 0.0)
    rank_all = _dot(sel.astype(BF16), upper.astype(BF16)) + carry_ref[:, 0:1]
    pad = ROUTE_SUBLANES - TOP_K
    ranks = [jnp.sum(jnp.where(e_iota == ik, rank_all, 0.0), axis=0, keepdims=True) for ik in idxs]
    idx_ref[...] = jnp.concatenate(idxs + [jnp.zeros((pad, tr), I32)], axis=0)
    gate_ref[...] = jnp.concatenate([e / denom for e in ex] + [jnp.zeros((pad, tr), F32)], axis=0)
    rank_ref[...] = jnp.concatenate(ranks + [jnp.zeros((pad, tr), F32)], axis=0).astype(I32)
    carry_ref[...] = carry_ref[...] + jnp.sum(sel, axis=1, keepdims=True)
    cnt_ref[...] = carry_ref[...]


def _route(logits_t):
    bsz, n_exp, seq = logits_t.shape
    tr = _tile(seq, ROUTE_TOKENS)
    nt = seq // tr
    tok = pl.BlockSpec((ROUTE_SUBLANES, tr), lambda b, i: (0, b * nt + i))
    shape = lambda dt: jax.ShapeDtypeStruct((ROUTE_SUBLANES, bsz * seq), dt)
    return pl.pallas_call(
        _route_kernel,
        grid=(bsz, nt),
        in_specs=[pl.BlockSpec((1, n_exp, tr), lambda b, i: (b, 0, i))],
        out_specs=[tok, tok, tok, pl.BlockSpec((n_exp, 128), lambda b, i: (0, 0))],
        out_shape=[shape(I32), shape(F32), shape(I32), jax.ShapeDtypeStruct((n_exp, 128), F32)],
        scratch_shapes=[pltpu.VMEM((n_exp, 128), F32)],
        compiler_params=_params(2),
        name="route_topk",
    )(logits_t)


def _dest_kernel(idx_ref, rank_ref, start_ref, dest_ref):
    idx = idx_ref[...]
    n_exp = start_ref.shape[0]
    tr = idx.shape[1]
    e_iota = lax.broadcasted_iota(I32, (n_exp, tr), 0)
    start = start_ref[:, 0:1]
    rows = [jnp.sum(jnp.where(e_iota == idx[k:k + 1, :], start, 0), axis=0, keepdims=True) for k in range(TOP_K)]
    rows.append(jnp.zeros((ROUTE_SUBLANES - TOP_K, tr), I32))
    dest_ref[...] = rank_ref[...] + jnp.concatenate(rows, axis=0)


def _dest_rows(idx_t, rank_t, pad_start):
    n_tok = idx_t.shape[1]
    tr = _tile(n_tok, DEST_TOKENS)
    tok = pl.BlockSpec((ROUTE_SUBLANES, tr), lambda i: (0, i))
    return pl.pallas_call(
        _dest_kernel,
        grid=(n_tok // tr,),
        in_specs=[tok, tok, pl.BlockSpec(pad_start.shape, lambda i: (0, 0))],
        out_specs=tok,
        out_shape=jax.ShapeDtypeStruct(idx_t.shape, I32),
        compiler_params=_params(1),
        name="dest_rows",
    )(idx_t, rank_t, pad_start)


def _relayout_copies(flat_ref, tiled_ref, sem, to_tiles):
    copies = []
    for s in range(ROW_TILE_SUBLANES):
        flat = flat_ref.at[:, pl.ds(s * LANES, LANES)]
        tiled = tiled_ref.at[:, s, :]
        copies.append(pltpu.make_async_copy(flat, tiled, sem) if to_tiles else pltpu.make_async_copy(tiled, flat, sem))
    return copies


def _dispatch_kernel(dest_ref, pend_ref, h_ref, xs_ref, rows_ref, zeros_ref, load_sems, row_sems):
    step = pl.program_id(0)
    last = pl.num_programs(0) - 1
    tt = rows_ref.shape[1]
    bm = zeros_ref.shape[0]
    slot = lax.rem(step, 3)
    slot_next = lax.rem(step + 1, 3)
    slot_prev = lax.rem(step + 2, 3)

    def load(tile, sl):
        return _relayout_copies(h_ref.at[pl.ds(tile * tt, tt)], rows_ref.at[sl], load_sems.at[sl], True)

    @pl.when(step == 0)
    def _():
        for c in load(0, 0):
            c.start()
        zeros_ref[...] = jnp.zeros_like(zeros_ref)

        def zero_copy(start):
            return pltpu.make_async_copy(zeros_ref, xs_ref.at[pl.ds(start, bm)], row_sems.at[0])

        for e in range(N_EXPERTS):
            zero_copy(jnp.maximum(pend_ref[e] - bm, 0)).start()
        for e in range(N_EXPERTS):
            zero_copy(0).wait()

        def zero_tail(j, carry):
            zero_copy(j * bm).start()
            zero_copy(0).wait()
            return carry

        lax.fori_loop(pend_ref[N_EXPERTS - 1] // bm, xs_ref.shape[0] // bm, zero_tail, 0)

    @pl.when(step < last)
    def _():
        for c in load(step + 1, slot_next):
            c.start()

    for c in load(step, slot):
        c.wait()

    def issue(t, carry):
        for k in range(TOP_K):
            pltpu.make_async_copy(rows_ref.at[slot, t], xs_ref.at[dest_ref[t * TOP_K + k]],
                                  row_sems.at[slot]).start(priority=k % 2)
        return carry

    lax.fori_loop(0, tt, issue, 0, unroll=4)

    def drain(sl):
        def body(t, carry):
            for _ in range(TOP_K):
                pltpu.make_async_copy(rows_ref.at[sl, 0], xs_ref.at[0], row_sems.at[sl]).wait()
            return carry

        lax.fori_loop(0, tt, body, 0, unroll=4)

    @pl.when(step > 0)
    def _():
        drain(slot_prev)

    @pl.when(step == last)
    def _():
        drain(slot)


def _dispatch(dest_flat, pad_end, h2, n_rows):
    n_tok, d = h2.shape
    assert d == ROW_TILE_SUBLANES * LANES
    tt = _tile(n_tok, DISPATCH_TOKENS)
    tile = (ROW_TILE_SUBLANES, LANES)
    return pl.pallas_call(
        _dispatch_kernel,
        grid=(n_tok // tt,),
        in_specs=[pl.BlockSpec((tt * TOP_K,), lambda i: (i,), memory_space=pltpu.SMEM),
                  pl.BlockSpec(memory_space=pltpu.SMEM),
                  pl.BlockSpec(memory_space=pl.ANY)],
        out_specs=pl.BlockSpec(memory_space=pl.ANY),
        out_shape=jax.ShapeDtypeStruct((n_rows,) + tile, F32),
        scratch_shapes=[pltpu.VMEM((3, tt) + tile, F32), pltpu.VMEM((EXPERT_ROWS,) + tile, F32),
                        pltpu.SemaphoreType.DMA((3,)), pltpu.SemaphoreType.DMA((3,))],
        compiler_params=_params(1),
        name="moe_dispatch",
    )(dest_flat, pad_end, h2)


def _expert_kernel(be_ref, nu_ref, xs_ref, wgu_ref, bgu_ref, wd_ref, bd_ref, ys_ref,
                   x_buf, y_buf, wgu_bf16, wd_bf16, in_sems, out_sems):
    step = pl.program_id(0)
    n_used = nu_ref[0]
    bm = x_buf.shape[1]
    de = wd_ref.shape[1]
    slot = step % 2

    def load(block, sl):
        return _relayout_copies(x_buf.at[sl], xs_ref.at[pl.ds(block * bm, bm)], in_sems.at[sl], False)

    def store(block, sl):
        return _relayout_copies(y_buf.at[sl], ys_ref.at[pl.ds(block * bm, bm)], out_sems.at[sl], True)

    @pl.when(step < n_used)
    def _():
        @pl.when(step == 0)
        def _():
            for c in load(0, 0):
                c.start()

        @pl.when(step + 1 < n_used)
        def _():
            for c in load(step + 1, 1 - slot):
                c.start()

        @pl.when((step == 0) | (be_ref[step] != be_ref[jnp.maximum(step - 1, 0)]))
        def _():
            wgu_bf16[...] = wgu_ref[0].astype(BF16)
            wd_bf16[...] = wd_ref[0].astype(BF16)

        for c in load(step, slot):
            c.wait()
        gu = _dot(x_buf[slot].astype(BF16), wgu_bf16[...]) + bgu_ref[0]
        gate = jnp.minimum(gu[:, :de], SWIGLU_LIMIT)
        up = jnp.clip(gu[:, de:], -SWIGLU_LIMIT, SWIGLU_LIMIT)
        glu = gate * jax.nn.sigmoid(SWIGLU_ALPHA * gate)
        y = _dot(((up + 1.0) * glu).astype(BF16), wd_bf16[...]) + bd_ref[0]

        @pl.when(step >= 2)
        def _():
            for c in store(step - 2, slot):
                c.wait()

        y_buf[slot] = y
        for c in store(step, slot):
            c.start()

        @pl.when(step == n_used - 1)
        def _():
            @pl.when(step >= 1)
            def _():
                for c in store(step - 1, 1 - slot):
                    c.wait()

            for c in store(step, slot):
                c.wait()

    @pl.when(step >= n_used)
    def _():
        y_buf[slot] = jnp.zeros(y_buf.shape[1:], F32)
        for c in store(step, slot):
            c.start()
        for c in store(step, slot):
            c.wait()


def _experts(block_expert, n_used, xs, w_gu, b_gu, w_d, b_d):
    n_rows = xs.shape[0]
    n_blocks = n_rows // EXPERT_ROWS
    _, d, de2 = w_gu.shape
    de = w_d.shape[1]
    per_expert = lambda *dims: pl.BlockSpec((1,) + dims, lambda i, be, nu: (be[i], 0, 0))
    return pl.pallas_call(
        _expert_kernel,
        grid_spec=pltpu.PrefetchScalarGridSpec(
            num_scalar_prefetch=2,
            grid=(n_blocks,),
            in_specs=[pl.BlockSpec(memory_space=pl.ANY), per_expert(d, de2), per_expert(1, de2),
                      per_expert(de, d), per_expert(1, d)],
            out_specs=pl.BlockSpec(memory_space=pl.ANY),
            scratch_shapes=[pltpu.VMEM((2, EXPERT_ROWS, d), F32), pltpu.VMEM((2, EXPERT_ROWS, d), F32),
                            pltpu.VMEM((d, de2), BF16), pltpu.VMEM((de, d), BF16),
                            pltpu.SemaphoreType.DMA((2,)), pltpu.SemaphoreType.DMA((2,))],
        ),
        out_shape=jax.ShapeDtypeStruct(xs.shape, F32),
        compiler_params=_params(1),
        name="moe_experts",
    )(block_expert, n_used, xs, w_gu, b_gu, w_d, b_d)


def _combine_kernel(idx_ref, next_ref, gate_ref, x1_ref, mod_ref, g_ref, ys_ref, o_ref, buf, sems):
    step = pl.program_id(0)
    tt, d = x1_ref.shape
    groups = tt // ROW_TILE_SUBLANES
    slot = step % 2

    def issue_gathers(rows_ref, sl):
        def body(i, carry):
            for j in range(ROW_TILE_SUBLANES):
                for k in range(TOP_K):
                    row = rows_ref[(i * ROW_TILE_SUBLANES + j) * TOP_K + k]
                    pltpu.make_async_copy(ys_ref.at[row], buf.at[sl, k, i, :, j, :], sems.at[sl]).start(priority=k % 2)
            return carry

        lax.fori_loop(0, groups, body, 0)

    @pl.when(step == 0)
    def _():
        issue_gathers(idx_ref, 0)

    @pl.when(step + 1 < pl.num_programs(0))
    def _():
        issue_gathers(next_ref, 1 - slot)

    def drain(i, carry):
        for _ in range(ROW_TILE_SUBLANES * TOP_K):
            pltpu.make_async_copy(ys_ref.at[0], buf.at[slot, 0, 0, :, 0, :], sems.at[slot]).wait()
        return carry

    lax.fori_loop(0, groups, drain, 0)

    eye = lax.broadcasted_iota(I32, (tt, tt), 0) == lax.broadcasted_iota(I32, (tt, tt), 1)
    gates = gate_ref[...]
    cols = [jnp.sum(jnp.where(eye, gates[k:k + 1, :], 0.0), axis=1, keepdims=True) for k in range(TOP_K)]
    chunks = []
    sq = jnp.zeros((tt, LANES), F32)
    for c in range(d // LANES):
        lanes = slice(c * LANES, (c + 1) * LANES)
        moe = cols[0] * buf[slot, 0, :, c].reshape(tt, LANES)
        for k in range(1, TOP_K):
            moe = moe + cols[k] * buf[slot, k, :, c].reshape(tt, LANES)
        x2 = x1_ref[:, lanes] + mod_ref[0, 5:6, lanes] * moe
        sq = sq + x2 * x2
        chunks.append(x2)
    scale = lax.rsqrt(jnp.sum(sq, axis=-1, keepdims=True) / d + EPS)
    for c, x2 in enumerate(chunks):
        lanes = slice(c * LANES, (c + 1) * LANES)
        o_ref[:, lanes] = x2 * scale * g_ref[:, lanes]


def _combine(dest_flat, gate_t, x1, mod3, g, ys, seq):
    n_tok, d = x1.shape
    tt = _tile(seq, COMBINE_TOKENS)
    per_seq = seq // tt
    n_steps = n_tok // tt
    idx_spec = lambda ahead: pl.BlockSpec((tt * TOP_K,), lambda i: (jnp.minimum(i + ahead, n_steps - 1),),
                                          memory_space=pltpu.SMEM)
    return pl.pallas_call(
        _combine_kernel,
        grid=(n_steps,),
        in_specs=[idx_spec(0), idx_spec(1),
                  pl.BlockSpec((ROUTE_SUBLANES, tt), lambda i: (0, i)),
                  pl.BlockSpec((tt, d), lambda i: (i, 0)),
                  pl.BlockSpec((1, 6, d), lambda i: (i // per_seq, 0, 0)),
                  pl.BlockSpec((1, d), lambda i: (0, 0)),
                  pl.BlockSpec(memory_space=pl.ANY)],
        out_specs=pl.BlockSpec((tt, d), lambda i: (i, 0)),
        out_shape=jax.ShapeDtypeStruct((n_tok, d), F32),
        scratch_shapes=[pltpu.VMEM((2, TOP_K, tt // ROW_TILE_SUBLANES, d // LANES, ROW_TILE_SUBLANES, LANES), F32),
                        pltpu.SemaphoreType.DMA((2,))],
        compiler_params=_params(1),
        name="moe_combine",
    )(dest_flat, dest_flat, gate_t, x1, mod3, g, ys)


def kernel(x, c, w_ada, b_ada, norm_mix_g, w_in, b_in, hy_conv_w, hy_conv_b, filt_w1, filt_b1, filt_w2, filt_b2, filt_w3, filt_b3, filt_w4, filt_freq, hy_d_skip, lambda_qk, attn_subln_g, w_hy_out, w_attn_out, w_mix_out, norm_ffn_g, w_router, b_router, w_gate_up, b_gate_up, w_down, b_down, final_norm_g):
    bsz, seq, d = x.shape
    depth = w_ada.shape[0]
    n_tok = bsz * seq
    row2 = lambda a: a.reshape(1, -1)

    cos_b, sin_b, seg_cos, seg_nsin = (jnp.asarray(t).astype(BF16) for t in _dft_tables(_tile(seq, HYENA_BLOCK)))
    z, z_rev = _filter_features(seq)
    min_decay = math.log(DECAY_TARGET) / SLOW_DECAY_PCT
    max_decay = math.log(DECAY_TARGET) / FAST_DECAY_PCT
    absdelta = jnp.abs(jnp.linspace(min_decay, max_decay, HYENA_WIDTH, dtype=F32))[None, :]
    kpos_a, kpos_b, qpos_a, qpos_b = (t.astype(BF16) for t in _alibi_tables(seq))

    n_blocks = -(-n_tok * TOP_K // EXPERT_ROWS) + N_EXPERTS
    n_rows = n_blocks * EXPERT_ROWS

    assert depth == 1, "LAMBDA_INIT is the first layer's"
    for l in range(depth):
        mod3 = _adaln_mod(c, w_ada[l], row2(b_ada[l])).reshape(bsz, 6, d)

        u_hy, q, k_a, k_b, v_ones, sg = _in_proj(x, mod3, row2(norm_mix_g[l]), w_in[l].astype(BF16), row2(b_in[l]),
                                                 kpos_a, kpos_b)

        w1p = jnp.pad(filt_w1[l], ((0, 128 - filt_w1.shape[1]), (0, 0)))
        kr, kq = _hyena_filters(z, z_rev, w1p, row2(filt_b1[l]), filt_w2[l], row2(filt_b2[l]), filt_w3[l],
                                row2(filt_b3[l]), filt_w4[l], row2(filt_freq[l]), absdelta, seg_cos, seg_nsin)
        y_hy = _hyena(u_hy, hy_conv_w[l], row2(hy_conv_b[l]), row2(hy_d_skip[l]), cos_b, sin_b, kr, kq)

        y_at = _diff_attention(q, k_a, k_b, v_ones, qpos_a, qpos_b, lambda_qk[l], row2(attn_subln_g[l]))

        x1, h2, logits_t = _merge(
            y_hy, y_at, sg, x, mod3, w_hy_out[l].astype(BF16), w_attn_out[l].astype(BF16),
            w_mix_out[l].astype(BF16), row2(norm_ffn_g[l]),
            jnp.pad(w_router[l], ((0, 0), (0, LANES - N_EXPERTS))), b_router[l].reshape(-1, 1))

        idx_t, gate_t, rank_t, counts = _route(logits_t)
        counts = counts[:, 0].astype(I32)
        padded = ((counts + EXPERT_ROWS - 1) // EXPERT_ROWS) * EXPERT_ROWS
        pad_end = jnp.cumsum(padded)
        pad_start = pad_end - padded
        dest_t = _dest_rows(idx_t, rank_t, jnp.broadcast_to(pad_start[:, None], (N_EXPERTS, 128)))
        dest_flat = dest_t[:TOP_K].T.reshape(-1)
        block_start = jnp.arange(n_blocks, dtype=I32) * EXPERT_ROWS
        block_expert = jnp.sum((block_start[:, None] >= pad_end[None, :]).astype(I32), axis=1)
        block_expert = jnp.minimum(block_expert, N_EXPERTS - 1)
        n_used = (pad_end[-1:] // EXPERT_ROWS).astype(I32)

        xs = _dispatch(dest_flat, pad_end.astype(I32), h2.reshape(n_tok, d), n_rows)
        ys = _experts(block_expert, n_used, xs, w_gate_up[l], b_gate_up[l][:, None, :],
                      w_down[l], b_down[l][:, None, :])
        x = _combine(dest_flat, gate_t, x1.reshape(n_tok, d), mod3, row2(final_norm_g), ys, seq).reshape(bsz, seq, d)
    return x
```

```python
import functools
import math

import numpy as np
import jax
import jax.numpy as jnp
from jax import lax
from jax.experimental import pallas as pl
from jax.experimental.pallas import tpu as pltpu

F32 = jnp.float32
BF16 = jnp.bfloat16
I32 = jnp.int32

EPS = 1e-5
HYENA_WIDTH = 512
FILTER_BANDS = 8
DECAY_TARGET = 1e-2
FAST_DECAY_PCT = 0.3
SLOW_DECAY_PCT = 1.5
ATTN_HEADS = 4
ATTN_HEAD_DIM = 64
ATTN_V_DIM = 2 * ATTN_HEAD_DIM
ATTN_WIDTH = ATTN_HEADS * ATTN_V_DIM
N_EXPERTS = 32
TOP_K = 4
SWIGLU_LIMIT = 7.0
SWIGLU_ALPHA = 1.702
LAMBDA_INIT = 0.8 - 0.6 * math.exp(-0.3 * 0)
ALIBI_SPLIT_LOG2 = 6

IN_PROJ_ROWS = 512
ATTN_Q_ROWS = 512
ATTN_BATCH_PER_STEP = 4
HYENA_BLOCK = 512
MERGE_ROWS = 512
ROUTE_TOKENS = 512
DEST_TOKENS = 2048
DISPATCH_TOKENS = 256
EXPERT_ROWS = 512
COMBINE_TOKENS = 256
ROUTE_SUBLANES = 8
ROW_TILE_SUBLANES = 8
LANES = 128

VMEM_LIMIT_BYTES = 56 * 1024 * 1024


def _tile(n, t):
    t = min(n, t)
    assert n % t == 0, (n, t)
    return t


def _params(n_axes):
    return pltpu.CompilerParams(
        dimension_semantics=("arbitrary",) * n_axes, vmem_limit_bytes=VMEM_LIMIT_BYTES
    )


def _split_bf16(a):
    hi = a.astype(BF16)
    lo = (a - hi.astype(F32)).astype(BF16)
    return hi, lo


_NN = (((1,), (0,)), ((), ()))
_NT = (((1,), (1,)), ((), ()))


def _dot3(a, b, dims=_NN):
    ah, al = _split_bf16(a)
    bh, bl = _split_bf16(b)
    d = lambda x, y: lax.dot_general(x, y, dims, preferred_element_type=F32)
    return d(ah, bh) + d(ah, bl) + d(al, bh)


def _dot(a, b):
    return jnp.dot(a, b, preferred_element_type=F32)


def _rms(x):
    return x * lax.rsqrt(jnp.mean(x * x, axis=-1, keepdims=True) + EPS)


def _mod_kernel(c_ref, w_ref, b_ref, o_ref):
    c = c_ref[...]
    o_ref[...] = _dot3(c * jax.nn.sigmoid(c), w_ref[...]) + b_ref[...]


def _adaln_mod(c, w, b):
    bsz, d = c.shape
    n = w.shape[1]
    tn = _tile(n, 1536)
    return pl.pallas_call(
        _mod_kernel,
        grid=(n // tn,),
        in_specs=[
            pl.BlockSpec((bsz, d), lambda j: (0, 0)),
            pl.BlockSpec((d, tn), lambda j: (0, j)),
            pl.BlockSpec((1, tn), lambda j: (0, j)),
        ],
        out_specs=pl.BlockSpec((bsz, tn), lambda j: (0, j)),
        out_shape=jax.ShapeDtypeStruct((bsz, n), F32),
        compiler_params=_params(1),
        name="adaln_mod",
    )(c, w, b)


def _alibi_tables(seq):
    slopes = 2.0 ** (-8.0 * jnp.arange(1, ATTN_HEADS + 1, dtype=F32) / ATTN_HEADS)
    pos = jnp.arange(seq, dtype=I32)
    hi = (jnp.right_shift(pos, ALIBI_SPLIT_LOG2) << ALIBI_SPLIT_LOG2).astype(F32)[:, None] * slopes[None, :]
    lo = (pos & ((1 << ALIBI_SPLIT_LOG2) - 1)).astype(F32)[:, None] * slopes[None, :]
    one = jnp.ones_like(hi)
    key4 = jnp.stack([hi, lo, one, one], axis=-1)
    qry4 = jnp.stack([one, one, -hi, -lo], axis=-1)

    def place(vals, first_lane):
        tab = jnp.zeros((seq, ATTN_HEADS, ATTN_V_DIM), F32).at[:, :, first_lane:first_lane + 4].set(vals)
        return tab.reshape(seq, ATTN_WIDTH)

    half = ATTN_HEAD_DIM
    return place(key4, half), place(key4, 0), place(qry4, half), place(qry4, 0)


def _in_proj_kernel(x_ref, mod_ref, g_ref, w_ref, b_ref, kpa_ref, kpb_ref,
                    uhy_ref, q_ref, ka_ref, kb_ref, v_ref, sg_ref, *, cols):
    x = x_ref[0]
    h = _rms(x) * g_ref[...]
    h = h * (1.0 + mod_ref[0, 1:2, :]) + mod_ref[0, 0:1, :]
    hb = h.astype(BF16)

    def proj(lo, hi):
        return _dot(hb, w_ref[:, lo:hi]) + b_ref[:, lo:hi]

    c0, c1, c2, c3, c4 = cols
    uhy_ref[0] = proj(0, c0).astype(BF16)
    q_ref[0] = (proj(c0, c1) * (1.0 / math.sqrt(ATTN_HEAD_DIM))).astype(BF16)
    k = proj(c1, c2).astype(BF16)
    first_half = (lax.broadcasted_iota(I32, (1, ATTN_WIDTH), 1) % ATTN_V_DIM) < ATTN_HEAD_DIM
    ka_ref[0] = jnp.where(first_half, k, kpa_ref[...])
    kb_ref[0] = jnp.where(first_half, kpb_ref[...], k)
    v = proj(c2, c3).astype(BF16)
    ones_col = jnp.where(lax.broadcasted_iota(I32, (v.shape[0], ATTN_V_DIM), 1) == 0, 1.0, 0.0).astype(BF16)
    pieces = []
    for hd in range(ATTN_HEADS):
        pieces += [v[:, hd * ATTN_V_DIM:(hd + 1) * ATTN_V_DIM], ones_col]
    v_ref[0] = jnp.concatenate(pieces, axis=1)
    sg_ref[0] = jax.nn.sigmoid(proj(c3, c4)).astype(BF16)


def _in_proj(x, mod3, g, w_bf16, b, kpos_a, kpos_b):
    bsz, seq, d = x.shape
    n = w_bf16.shape[1]
    hw3 = 3 * HYENA_WIDTH
    cols = (hw3, hw3 + ATTN_WIDTH, hw3 + 2 * ATTN_WIDTH, hw3 + 3 * ATTN_WIDTH, n)
    tm = _tile(seq, IN_PROJ_ROWS)
    row = lambda width: pl.BlockSpec((1, tm, width), lambda bi, i: (bi, i, 0))
    out = lambda width: jax.ShapeDtypeStruct((bsz, seq, width), BF16)
    pos = pl.BlockSpec((tm, ATTN_WIDTH), lambda bi, i: (i, 0))
    widths = (hw3, ATTN_WIDTH, ATTN_WIDTH, ATTN_WIDTH, 2 * ATTN_WIDTH, 2 * d)
    return pl.pallas_call(
        functools.partial(_in_proj_kernel, cols=cols),
        grid=(bsz, seq // tm),
        in_specs=[
            row(d),
            pl.BlockSpec((1, 6, d), lambda bi, i: (bi, 0, 0)),
            pl.BlockSpec((1, d), lambda bi, i: (0, 0)),
            pl.BlockSpec((d, n), lambda bi, i: (0, 0)),
            pl.BlockSpec((1, n), lambda bi, i: (0, 0)),
            pos, pos,
        ],
        out_specs=[row(w) for w in widths],
        out_shape=[out(w) for w in widths],
        compiler_params=_params(2),
        name="in_proj",
    )(x, mod3, g, w_bf16, b, kpos_a, kpos_b)


@functools.lru_cache(maxsize=None)
def _dft_tables(p):
    n_fft = 2 * p
    f = np.arange(p, dtype=np.int64)[:, None]
    ang = lambda n: 2.0 * np.pi * ((f * n[None, :]) % n_fft).astype(np.float64) / n_fft
    n_blk = np.arange(p, dtype=np.int64)
    n_seg = np.arange(-p, p, dtype=np.int64)
    keep = (n_seg != -p)[None, :]
    f32 = lambda a: a.astype(np.float32)
    return (f32(np.cos(ang(n_blk))), f32(np.sin(ang(n_blk))),
            f32(np.cos(ang(n_seg)) * keep), f32(-np.sin(ang(n_seg)) * keep))


def _filter_features(seq):
    t = jnp.linspace(0.0, 1.0, seq, dtype=F32)[:, None]
    w = 2.0 * math.pi * jnp.arange(seq, dtype=F32)[:, None] / seq
    fr = jnp.linspace(1e-4, FILTER_BANDS - 1, FILTER_BANDS, dtype=F32)[None, :]
    z = jnp.concatenate([t, jnp.cos(fr * w), -jnp.sin(fr * w)], axis=-1)
    z = jnp.pad(z, ((0, 0), (0, 128 - z.shape[1])))
    z_rev = jnp.concatenate([z[:1], z[:0:-1]], axis=0)
    return z, z_rev


def _alternating(rows):
    return jnp.where((rows & 1) == 0, 1.0, -1.0).astype(F32)


def _filter_kernel(z_ref, zrev_ref, w1_ref, b1_ref, w2_ref, b2_ref, w3_ref, b3_ref, w4_ref, fq_ref, ad_ref,
                   segcos_ref, segsin_ref, kr_ref, kq_ref, taps_ref):
    seq = z_ref.shape[0]
    width = ad_ref.shape[1]
    p = segcos_ref.shape[0]
    fq = fq_ref[...]

    def taps(feat, lo):
        h = jnp.sin(fq * (_dot3(feat, w1_ref[...]) + b1_ref[...]))
        h = jnp.sin(fq * (_dot3(h, w2_ref[...]) + b2_ref[...]))
        h = jnp.sin(fq * (_dot3(h, w3_ref[...]) + b3_ref[...]))
        return _dot3(h, w4_ref[:, lo:lo + width]) * jnp.exp(-feat[:, 0:1] * ad_ref[...])

    lag_is_minus_l = lax.broadcasted_iota(I32, (seq, 1), 0) == 0
    taps_ref[0:seq, :] = jnp.where(lag_is_minus_l, 0.0, taps(zrev_ref[...], width))
    taps_ref[seq:2 * seq, :] = taps(z_ref[...], 0)

    rows = lax.broadcasted_iota(I32, (p, 1), 0)
    first = rows == 0
    scale = jnp.where(first, 0.5 / p, 1.0 / p)
    lag = lax.broadcasted_iota(I32, (2 * p, 1), 0)
    alt = jnp.where(lag == 0, 0.0, _alternating(lag))
    n_seg = kr_ref.shape[0]
    for s in range(n_seg):
        d = s - n_seg // 2
        seg = taps_ref[(d - 1) * p + seq:(d + 1) * p + seq, :]
        seg_b = seg.astype(BF16)
        kr = _dot(segcos_ref[...], seg_b)
        kq = jnp.where(first, jnp.sum(alt * seg, axis=0, keepdims=True), _dot(segsin_ref[...], seg_b))
        kr_ref[s] = kr * scale
        kq_ref[s] = kq * scale


def _hyena_filters(z, z_rev, w1, b1, w2, b2, w3, b3, w4, fq, absdelta, seg_cos, seg_nsin):
    seq = z.shape[0]
    width = absdelta.shape[1]
    p = seg_cos.shape[0]
    n_seg = 2 * (seq // p) - 1
    args = (z, z_rev, w1, b1, w2, b2, w3, b3, w4, fq, absdelta, seg_cos, seg_nsin)
    full = lambda a: pl.BlockSpec(a.shape, lambda i: (0,) * a.ndim)
    return pl.pallas_call(
        _filter_kernel,
        grid=(1,),
        in_specs=[full(a) for a in args],
        out_specs=[pl.BlockSpec((n_seg, p, width), lambda i: (0, 0, 0))] * 2,
        out_shape=[jax.ShapeDtypeStruct((n_seg, p, width), F32)] * 2,
        scratch_shapes=[pltpu.VMEM((2 * seq, width), F32)],
        compiler_params=_params(1),
        name="hyena_filters",
    )(*args)


def _hyena_kernel(x0_ref, x1_ref, v_ref, w0_ref, w1_ref, wv_ref, b0_ref, b1_ref, bv_ref, dskip_ref,
                  cos_ref, sin_ref, kr_ref, kq_ref, o_ref):
    seq = x0_ref.shape[1]
    rows = lax.broadcasted_iota(I32, (seq, 1), 0)
    first = rows == 0
    last = rows == seq - 1
    alt = _alternating(rows)

    def short_conv(u_ref, w_ref, b_ref):
        u = u_ref[0].astype(F32)
        prev = jnp.where(first, 0.0, pltpu.roll(u, 1, 0))
        nxt = jnp.where(last, 0.0, pltpu.roll(u, seq - 1, 0))
        w = w_ref[...]
        return prev * w[0:1] + u * w[1:2] + nxt * w[2:3] + b_ref[...]

    x0 = short_conv(x0_ref, w0_ref, b0_ref)
    x1 = short_conv(x1_ref, w1_ref, b1_ref)
    v = short_conv(v_ref, wv_ref, bv_ref) * x1
    vb = v.astype(BF16)
    cos_m = cos_ref[...]
    sin_m = sin_ref[...]
    p = cos_m.shape[0]
    nb = seq // p
    first_p = first[:p]
    alt_p = alt[:p]
    vr, vq = [], []
    for j in range(nb):
        blk = slice(j * p, (j + 1) * p)
        vr.append(_dot(cos_m, vb[blk]))
        vq.append(jnp.where(first_p, jnp.sum(alt_p * v[blk], axis=0, keepdims=True), _dot(sin_m, vb[blk])))
    y_blocks = []
    for i in range(nb):
        yr = yq = dc = nyq = None
        for j in range(nb):
            kr = kr_ref[i - j + nb - 1]
            kq = kq_ref[i - j + nb - 1]
            terms = (vr[j] * kr + vq[j] * kq, vq[j] * kr - vr[j] * kq, vr[j][0:1] * kr[0:1], vq[j][0:1] * kq[0:1])
            yr, yq, dc, nyq = terms if j == 0 else (yr + terms[0], yq + terms[1], dc + terms[2], nyq + terms[3])
        yr = jnp.where(first_p, dc, yr)
        yq = jnp.where(first_p, nyq, yq)
        y_blocks.append(_dot(cos_m, yr.astype(BF16)) + _dot(sin_m, yq.astype(BF16)) + alt_p * nyq)
    y = jnp.concatenate(y_blocks, axis=0)
    o_ref[0] = ((y + v * dskip_ref[...]) * x0).astype(BF16)


def _hyena(u_hy, conv_w, conv_b, d_skip, cos_b, sin_b, kr, kq):
    bsz, seq, _ = u_hy.shape
    width = d_skip.shape[1]
    p = cos_b.shape[0]
    cw = _tile(width, 256)
    nc = width // cw
    u_spec = lambda part: pl.BlockSpec((1, seq, cw), lambda c, b: (b, 0, part * nc + c))
    w_spec = lambda part: pl.BlockSpec((3, cw), lambda c, b: (0, part * nc + c))
    b_spec = lambda part: pl.BlockSpec((1, cw), lambda c, b: (0, part * nc + c))
    full = pl.BlockSpec((p, p), lambda c, b: (0, 0))
    k_spec = pl.BlockSpec((kr.shape[0], p, cw), lambda c, b: (0, 0, c))
    return pl.pallas_call(
        _hyena_kernel,
        grid=(nc, bsz),
        in_specs=[u_spec(0), u_spec(1), u_spec(2), w_spec(0), w_spec(1), w_spec(2),
                  b_spec(0), b_spec(1), b_spec(2), pl.BlockSpec((1, cw), lambda c, b: (0, c)),
                  full, full, k_spec, k_spec],
        out_specs=pl.BlockSpec((1, seq, cw), lambda c, b: (b, 0, c)),
        out_shape=jax.ShapeDtypeStruct((bsz, seq, width), BF16),
        compiler_params=_params(2),
        name="hyena",
    )(u_hy, u_hy, u_hy, conv_w, conv_w, conv_w, conv_b, conv_b, conv_b, d_skip, cos_b, sin_b, kr, kq)


def _attn_kernel(q_ref, ka_ref, kb_ref, v_ref, qpa_ref, qpb_ref, lq_ref, g_ref, o_ref, s_ref, bias_ref):
    head = pl.program_id(0)
    qi = pl.program_id(1)
    tq = q_ref.shape[1]
    seq = ka_ref.shape[1]
    nk = seq // tq

    @pl.when(pl.program_id(2) == 0)
    def _():
        slope = jnp.where(head == 0, 0.25, jnp.where(head == 1, 0.0625, jnp.where(head == 2, 0.015625, 0.00390625)))
        r = lax.broadcasted_iota(I32, (tq, tq), 0)
        c = lax.broadcasted_iota(I32, (tq, tq), 1)
        bias_ref[...] = jnp.abs(r - c).astype(F32) * (-slope)

    first_half = lax.broadcasted_iota(I32, (1, ATTN_V_DIM), 1) < ATTN_HEAD_DIM
    maps = ((ka_ref, first_half, qpa_ref[...]), (kb_ref, ~first_half, qpb_ref[...]))
    starts = [pl.multiple_of(lax.rem(qi + r, nk) * tq, tq) for r in range(nk)]
    lq = lq_ref[...]
    lam = (jnp.exp(jnp.sum(lq[0:1] * lq[1:2], axis=-1, keepdims=True))
           - jnp.exp(jnp.sum(lq[2:3] * lq[3:4], axis=-1, keepdims=True)) + LAMBDA_INIT)

    for item in range(q_ref.shape[0]):
        q = q_ref[item]
        zero = jnp.zeros_like(q)
        sset = item % 2
        row_max = []
        for m, (k_ref, own, q_pos) in enumerate(maps):
            q_diag = jnp.where(own, q, zero)
            q_left = jnp.where(own, q, q_pos)
            q_right = jnp.where(own, q, -q_pos)
            for r in range(nk):
                kb = k_ref[item, pl.ds(starts[r], tq), :]
                if r == 0:
                    s = lax.dot_general(q_diag, kb, _NT, preferred_element_type=F32) + bias_ref[...]
                else:
                    q_side = jnp.where(lax.rem(qi + r, nk) < qi, q_left, q_right)
                    s = lax.dot_general(q_side, kb, _NT, preferred_element_type=F32)
                s_ref[sset, m, r] = s
                blk_max = jnp.max(s, axis=-1, keepdims=True)
                if r == 0:
                    row_max.append(blk_max)
                else:
                    row_max[m] = jnp.maximum(row_max[m], blk_max)

        outs = []
        for m in range(2):
            acc = jnp.zeros((tq, 2 * ATTN_V_DIM), F32)
            for r in range(nk):
                p = jnp.exp(s_ref[sset, m, r] - row_max[m]).astype(BF16)
                acc = acc + _dot(p, v_ref[item, pl.ds(starts[r], tq), :])
            outs.append(acc[:, :ATTN_V_DIM] / acc[:, ATTN_V_DIM:ATTN_V_DIM + 1])

        o = outs[0] - lam * outs[1]
        o_ref[item] = (_rms(o) * g_ref[...] * (1.0 - LAMBDA_INIT)).astype(BF16)


def _diff_attention(q, k_a, k_b, v_ones, qpos_a, qpos_b, lambda_qk, subln_g):
    bsz, seq, _ = q.shape
    tq = _tile(seq, ATTN_Q_ROWS)
    nbat = _tile(bsz, ATTN_BATCH_PER_STEP)
    k_spec = pl.BlockSpec((nbat, seq, ATTN_V_DIM), lambda h, i, b: (b, 0, h))
    v_spec = pl.BlockSpec((nbat, seq, 2 * ATTN_V_DIM), lambda h, i, b: (b, 0, h))
    q_spec = pl.BlockSpec((nbat, tq, ATTN_V_DIM), lambda h, i, b: (b, i, h))
    qpos_spec = pl.BlockSpec((tq, ATTN_V_DIM), lambda h, i, b: (i, h))
    return pl.pallas_call(
        _attn_kernel,
        grid=(ATTN_HEADS, seq // tq, bsz // nbat),
        in_specs=[q_spec, k_spec, k_spec, v_spec, qpos_spec, qpos_spec,
                  pl.BlockSpec(lambda_qk.shape, lambda h, i, b: (0, 0)),
                  pl.BlockSpec(subln_g.shape, lambda h, i, b: (0, 0))],
        out_specs=q_spec,
        out_shape=jax.ShapeDtypeStruct((bsz, seq, ATTN_WIDTH), BF16),
        scratch_shapes=[pltpu.VMEM((2, 2, seq // tq, tq, tq), F32), pltpu.VMEM((tq, tq), F32)],
        compiler_params=_params(3),
        name="diff_attention",
    )(q, k_a, k_b, v_ones, qpos_a, qpos_b, lambda_qk, subln_g)


def _merge_kernel(yhy_ref, yat_ref, sg_ref, x_ref, mod_ref, why_ref, wat_ref, wmix_ref, g_ref,
                  wr_ref, br_ref, x1_ref, h2_ref, lg_ref):
    d = x_ref.shape[2]
    sg = sg_ref[0]
    merged = (sg[:, :d].astype(F32) * _dot(yhy_ref[0], why_ref[...])
              + sg[:, d:].astype(F32) * _dot(yat_ref[0], wat_ref[...]))
    x1 = x_ref[0] + mod_ref[0, 2:3, :] * _dot(merged.astype(BF16), wmix_ref[...])
    x1_ref[0] = x1
    h2 = _rms(x1) * g_ref[...]
    h2 = h2 * (1.0 + mod_ref[0, 4:5, :]) + mod_ref[0, 3:4, :]
    h2_ref[0] = h2
    lg_ref[0] = _dot3(wr_ref[...], h2, _NT) + br_ref[...]


def _merge(y_hy, y_at, sg, x, mod3, w_hy, w_at, w_mix, g, w_router, b_router):
    bsz, seq, d = x.shape
    tm = _tile(seq, MERGE_ROWS)
    row = lambda width: pl.BlockSpec((1, tm, width), lambda b, i: (b, i, 0))
    full = lambda a: pl.BlockSpec(a.shape, lambda b, i: (0,) * a.ndim)
    return pl.pallas_call(
        _merge_kernel,
        grid=(bsz, seq // tm),
        in_specs=[row(y_hy.shape[2]), row(y_at.shape[2]), row(2 * d), row(d),
                  pl.BlockSpec((1, 6, d), lambda b, i: (b, 0, 0)),
                  full(w_hy), full(w_at), full(w_mix), full(g), full(w_router), full(b_router)],
        out_specs=[row(d), row(d), pl.BlockSpec((1, N_EXPERTS, tm), lambda b, i: (b, 0, i))],
        out_shape=[jax.ShapeDtypeStruct((bsz, seq, d), F32), jax.ShapeDtypeStruct((bsz, seq, d), F32),
                   jax.ShapeDtypeStruct((bsz, N_EXPERTS, seq), F32)],
        compiler_params=_params(2),
        name="merge_router",
    )(y_hy, y_at, sg, x, mod3, w_hy, w_at, w_mix, g, w_router, b_router)


def _route_kernel(lg_ref, idx_ref, gate_ref, rank_ref, cnt_ref, carry_ref):
    step = pl.program_id(0) * pl.num_programs(1) + pl.program_id(1)

    @pl.when(step == 0)
    def _():
        carry_ref[...] = jnp.zeros_like(carry_ref)

    work = lg_ref[0]
    n_exp, tr = work.shape
    e_iota = lax.broadcasted_iota(I32, (n_exp, tr), 0)
    vals, idxs = [], []
    for _ in range(TOP_K):
        m = jnp.max(work, axis=0, keepdims=True)
        ik = jnp.min(jnp.where(work == m, e_iota, n_exp), axis=0, keepdims=True)
        vals.append(m)
        idxs.append(ik)
        work = jnp.where(e_iota == ik, -jnp.inf, work)
    ex = [jnp.exp(val - vals[0]) for val in vals]
    denom = ex[0] + ex[1] + ex[2] + ex[3]
    sel = jnp.zeros((n_exp, tr), F32)
    for ik in idxs:
        sel = sel + jnp.where(e_iota == ik, 1.0, 0.0)
    upper = jnp.where(lax.broadcasted_iota(I32, (tr, tr), 0) < lax.broadcasted_iota(I32, (tr, tr), 1), 1.0, 0.0)
    rank_all = _dot(sel.astype(BF16), upper.astype(BF16)) + carry_ref[:, 0:1]
    pad = ROUTE_SUBLANES - TOP_K
    ranks = [jnp.sum(jnp.where(e_iota == ik, rank_all, 0.0), axis=0, keepdims=True) for ik in idxs]
    idx_ref[...] = jnp.concatenate(idxs + [jnp.zeros((pad, tr), I32)], axis=0)
    gate_ref[...] = jnp.concatenate([e / denom for e in ex] + [jnp.zeros((pad, tr), F32)], axis=0)
    rank_ref[...] = jnp.concatenate(ranks + [jnp.zeros((pad, tr), F32)], axis=0).astype(I32)
    carry_ref[...] = carry_ref[...] + jnp.sum(sel, axis=1, keepdims=True)
    cnt_ref[...] = carry_ref[...]


def _route(logits_t):
    bsz, n_exp, seq = logits_t.shape
    tr = _tile(seq, ROUTE_TOKENS)
    nt = seq // tr
    tok = pl.BlockSpec((ROUTE_SUBLANES, tr), lambda b, i: (0, b * nt + i))
    shape = lambda dt: jax.ShapeDtypeStruct((ROUTE_SUBLANES, bsz * seq), dt)
    return pl.pallas_call(
        _route_kernel,
        grid=(bsz, nt),
        in_specs=[pl.BlockSpec((1, n_exp, tr), lambda b, i: (b, 0, i))],
        out_specs=[tok, tok, tok, pl.BlockSpec((n_exp, 128), lambda b, i: (0, 0))],
        out_shape=[shape(I32), shape(F32), shape(I32), jax.ShapeDtypeStruct((n_exp, 128), F32)],
        scratch_shapes=[pltpu.VMEM((n_exp, 128), F32)],
        compiler_params=_params(2),
        name="route_topk",
    )(logits_t)


def _dest_kernel(idx_ref, rank_ref, start_ref, dest_ref):
    idx = idx_ref[...]
    n_exp = start_ref.shape[0]
    tr = idx.shape[1]
    e_iota = lax.broadcasted_iota(I32, (n_exp, tr), 0)
    start = start_ref[:, 0:1]
    rows = [jnp.sum(jnp.where(e_iota == idx[k:k + 1, :], start, 0), axis=0, keepdims=True) for k in range(TOP_K)]
    rows.append(jnp.zeros((ROUTE_SUBLANES - TOP_K, tr), I32))
    dest_ref[...] = rank_ref[...] + jnp.concatenate(rows, axis=0)


def _dest_rows(idx_t, rank_t, pad_start):
    n_tok = idx_t.shape[1]
    tr = _tile(n_tok, DEST_TOKENS)
    tok = pl.BlockSpec((ROUTE_SUBLANES, tr), lambda i: (0, i))
    return pl.pallas_call(
        _dest_kernel,
        grid=(n_tok // tr,),
        in_specs=[tok, tok, pl.BlockSpec(pad_start.shape, lambda i: (0, 0))],
        out_specs=tok,
        out_shape=jax.ShapeDtypeStruct(idx_t.shape, I32),
        compiler_params=_params(1),
        name="dest_rows",
    )(idx_t, rank_t, pad_start)


def _relayout_copies(flat_ref, tiled_ref, sem, to_tiles):
    copies = []
    for s in range(ROW_TILE_SUBLANES):
        flat = flat_ref.at[:, pl.ds(s * LANES, LANES)]
        tiled = tiled_ref.at[:, s, :]
        copies.append(pltpu.make_async_copy(flat, tiled, sem) if to_tiles else pltpu.make_async_copy(tiled, flat, sem))
    return copies


def _dispatch_kernel(dest_ref, pend_ref, h_ref, xs_ref, rows_ref, zeros_ref, load_sems, row_sems):
    step = pl.program_id(0)
    last = pl.num_programs(0) - 1
    tt = rows_ref.shape[1]
    bm = zeros_ref.shape[0]
    slot = lax.rem(step, 3)
    slot_next = lax.rem(step + 1, 3)
    slot_prev = lax.rem(step + 2, 3)

    def load(tile, sl):
        return _relayout_copies(h_ref.at[pl.ds(tile * tt, tt)], rows_ref.at[sl], load_sems.at[sl], True)

    @pl.when(step == 0)
    def _():
        for c in load(0, 0):
            c.start()
        zeros_ref[...] = jnp.zeros_like(zeros_ref)

        def zero_copy(start):
            return pltpu.make_async_copy(zeros_ref, xs_ref.at[pl.ds(start, bm)], row_sems.at[0])

        for e in range(N_EXPERTS):
            zero_copy(jnp.maximum(pend_ref[e] - bm, 0)).start()
        for e in range(N_EXPERTS):
            zero_copy(0).wait()

        def zero_tail(j, carry):
            zero_copy(j * bm).start()
            zero_copy(0).wait()
            return carry

        lax.fori_loop(pend_ref[N_EXPERTS - 1] // bm, xs_ref.shape[0] // bm, zero_tail, 0)

    @pl.when(step < last)
    def _():
        for c in load(step + 1, slot_next):
            c.start()

    for c in load(step, slot):
        c.wait()

    def issue(t, carry):
        for k in range(TOP_K):
            pltpu.make_async_copy(rows_ref.at[slot, t], xs_ref.at[dest_ref[t * TOP_K + k]],
                                  row_sems.at[slot]).start(priority=k % 2)
        return carry

    lax.fori_loop(0, tt, issue, 0, unroll=4)

    def drain(sl):
        def body(t, carry):
            for _ in range(TOP_K):
                pltpu.make_async_copy(rows_ref.at[sl, 0], xs_ref.at[0], row_sems.at[sl]).wait()
            return carry

        lax.fori_loop(0, tt, body, 0, unroll=4)

    @pl.when(step > 0)
    def _():
        drain(slot_prev)

    @pl.when(step == last)
    def _():
        drain(slot)


def _dispatch(dest_flat, pad_end, h2, n_rows):
    n_tok, d = h2.shape
    assert d == ROW_TILE_SUBLANES * LANES
    tt = _tile(n_tok, DISPATCH_TOKENS)
    tile = (ROW_TILE_SUBLANES, LANES)
    return pl.pallas_call(
        _dispatch_kernel,
        grid=(n_tok // tt,),
        in_specs=[pl.BlockSpec((tt * TOP_K,), lambda i: (i,), memory_space=pltpu.SMEM),
                  pl.BlockSpec(memory_space=pltpu.SMEM),
                  pl.BlockSpec(memory_space=pl.ANY)],
        out_specs=pl.BlockSpec(memory_space=pl.ANY),
        out_shape=jax.ShapeDtypeStruct((n_rows,) + tile, F32),
        scratch_shapes=[pltpu.VMEM((3, tt) + tile, F32), pltpu.VMEM((EXPERT_ROWS,) + tile, F32),
                        pltpu.SemaphoreType.DMA((3,)), pltpu.SemaphoreType.DMA((3,))],
        compiler_params=_params(1),
        name="moe_dispatch",
    )(dest_flat, pad_end, h2)


def _expert_kernel(be_ref, nu_ref, xs_ref, wgu_ref, bgu_ref, wd_ref, bd_ref, ys_ref,
                   x_buf, y_buf, wgu_bf16, wd_bf16, in_sems, out_sems):
    step = pl.program_id(0)
    n_used = nu_ref[0]
    bm = x_buf.shape[1]
    de = wd_ref.shape[1]
    slot = step % 2

    def load(block, sl):
        return _relayout_copies(x_buf.at[sl], xs_ref.at[pl.ds(block * bm, bm)], in_sems.at[sl], False)

    def store(block, sl):
        return _relayout_copies(y_buf.at[sl], ys_ref.at[pl.ds(block * bm, bm)], out_sems.at[sl], True)

    @pl.when(step < n_used)
    def _():
        @pl.when(step == 0)
        def _():
            for c in load(0, 0):
                c.start()

        @pl.when(step + 1 < n_used)
        def _():
            for c in load(step + 1, 1 - slot):
                c.start()

        @pl.when((step == 0) | (be_ref[step] != be_ref[jnp.maximum(step - 1, 0)]))
        def _():
            wgu_bf16[...] = wgu_ref[0].astype(BF16)
            wd_bf16[...] = wd_ref[0].astype(BF16)

        for c in load(step, slot):
            c.wait()
        gu = _dot(x_buf[slot].astype(BF16), wgu_bf16[...]) + bgu_ref[0]
        gate = jnp.minimum(gu[:, :de], SWIGLU_LIMIT)
        up = jnp.clip(gu[:, de:], -SWIGLU_LIMIT, SWIGLU_LIMIT)
        glu = gate * jax.nn.sigmoid(SWIGLU_ALPHA * gate)
        y = _dot(((up + 1.0) * glu).astype(BF16), wd_bf16[...]) + bd_ref[0]

        @pl.when(step >= 2)
        def _():
            for c in store(step - 2, slot):
                c.wait()

        y_buf[slot] = y
        for c in store(step, slot):
            c.start()

        @pl.when(step == n_used - 1)
        def _():
            @pl.when(step >= 1)
            def _():
                for c in store(step - 1, 1 - slot):
                    c.wait()

            for c in store(step, slot):
                c.wait()

    @pl.when(step >= n_used)
    def _():
        y_buf[slot] = jnp.zeros(y_buf.shape[1:], F32)
        for c in store(step, slot):
            c.start()
        for c in store(step, slot):
            c.wait()


def _experts(block_expert, n_used, xs, w_gu, b_gu, w_d, b_d):
    n_rows = xs.shape[0]
    n_blocks = n_rows // EXPERT_ROWS
    _, d, de2 = w_gu.shape
    de = w_d.shape[1]
    per_expert = lambda *dims: pl.BlockSpec((1,) + dims, lambda i, be, nu: (be[i], 0, 0))
    return pl.pallas_call(
        _expert_kernel,
        grid_spec=pltpu.PrefetchScalarGridSpec(
            num_scalar_prefetch=2,
            grid=(n_blocks,),
            in_specs=[pl.BlockSpec(memory_space=pl.ANY), per_expert(d, de2), per_expert(1, de2),
                      per_expert(de, d), per_expert(1, d)],
            out_specs=pl.BlockSpec(memory_space=pl.ANY),
            scratch_shapes=[pltpu.VMEM((2, EXPERT_ROWS, d), F32), pltpu.VMEM((2, EXPERT_ROWS, d), F32),
                            pltpu.VMEM((d, de2), BF16), pltpu.VMEM((de, d), BF16),
                            pltpu.SemaphoreType.DMA((2,)), pltpu.SemaphoreType.DMA((2,))],
        ),
        out_shape=jax.ShapeDtypeStruct(xs.shape, F32),
        compiler_params=_params(1),
        name="moe_experts",
    )(block_expert, n_used, xs, w_gu, b_gu, w_d, b_d)


def _combine_kernel(idx_ref, next_ref, gate_ref, x1_ref, mod_ref, g_ref, ys_ref, o_ref, buf, sems):
    step = pl.program_id(0)
    tt, d = x1_ref.shape
    groups = tt // ROW_TILE_SUBLANES
    slot = step % 2

    def issue_gathers(rows_ref, sl):
        def body(i, carry):
            for j in range(ROW_TILE_SUBLANES):
                for k in range(TOP_K):
                    row = rows_ref[(i * ROW_TILE_SUBLANES + j) * TOP_K + k]
                    pltpu.make_async_copy(ys_ref.at[row], buf.at[sl, k, i, :, j, :], sems.at[sl]).start(priority=k % 2)
            return carry

        lax.fori_loop(0, groups, body, 0)

    @pl.when(step == 0)
    def _():
        issue_gathers(idx_ref, 0)

    @pl.when(step + 1 < pl.num_programs(0))
    def _():
        issue_gathers(next_ref, 1 - slot)

    def drain(i, carry):
        for _ in range(ROW_TILE_SUBLANES * TOP_K):
            pltpu.make_async_copy(ys_ref.at[0], buf.at[slot, 0, 0, :, 0, :], sems.at[slot]).wait()
        return carry

    lax.fori_loop(0, groups, drain, 0)

    eye = lax.broadcasted_iota(I32, (tt, tt), 0) == lax.broadcasted_iota(I32, (tt, tt), 1)
    gates = gate_ref[...]
    cols = [jnp.sum(jnp.where(eye, gates[k:k + 1, :], 0.0), axis=1, keepdims=True) for k in range(TOP_K)]
    chunks = []
    sq = jnp.zeros((tt, LANES), F32)
    for c in range(d // LANES):
        lanes = slice(c * LANES, (c + 1) * LANES)
        moe = cols[0] * buf[slot, 0, :, c].reshape(tt, LANES)
        for k in range(1, TOP_K):
            moe = moe + cols[k] * buf[slot, k, :, c].reshape(tt, LANES)
        x2 = x1_ref[:, lanes] + mod_ref[0, 5:6, lanes] * moe
        sq = sq + x2 * x2
        chunks.append(x2)
    scale = lax.rsqrt(jnp.sum(sq, axis=-1, keepdims=True) / d + EPS)
    for c, x2 in enumerate(chunks):
        lanes = slice(c * LANES, (c + 1) * LANES)
        o_ref[:, lanes] = x2 * scale * g_ref[:, lanes]


def _combine(dest_flat, gate_t, x1, mod3, g, ys, seq):
    n_tok, d = x1.shape
    tt = _tile(seq, COMBINE_TOKENS)
    per_seq = seq // tt
    n_steps = n_tok // tt
    idx_spec = lambda ahead: pl.BlockSpec((tt * TOP_K,), lambda i: (jnp.minimum(i + ahead, n_steps - 1),),
                                          memory_space=pltpu.SMEM)
    return pl.pallas_call(
        _combine_kernel,
        grid=(n_steps,),
        in_specs=[idx_spec(0), idx_spec(1),
                  pl.BlockSpec((ROUTE_SUBLANES, tt), lambda i: (0, i)),
                  pl.BlockSpec((tt, d), lambda i: (i, 0)),
                  pl.BlockSpec((1, 6, d), lambda i: (i // per_seq, 0, 0)),
                  pl.BlockSpec((1, d), lambda i: (0, 0)),
                  pl.BlockSpec(memory_space=pl.ANY)],
        out_specs=pl.BlockSpec((tt, d), lambda i: (i, 0)),
        out_shape=jax.ShapeDtypeStruct((n_tok, d), F32),
        scratch_shapes=[pltpu.VMEM((2, TOP_K, tt // ROW_TILE_SUBLANES, d // LANES, ROW_TILE_SUBLANES, LANES), F32),
                        pltpu.SemaphoreType.DMA((2,))],
        compiler_params=_params(1),
        name="moe_combine",
    )(dest_flat, dest_flat, gate_t, x1, mod3, g, ys)


def kernel(x, c, w_ada, b_ada, norm_mix_g, w_in, b_in, hy_conv_w, hy_conv_b, filt_w1, filt_b1, filt_w2, filt_b2, filt_w3, filt_b3, filt_w4, filt_freq, hy_d_skip, lambda_qk, attn_subln_g, w_hy_out, w_attn_out, w_mix_out, norm_ffn_g, w_router, b_router, w_gate_up, b_gate_up, w_down, b_down, final_norm_g):
    bsz, seq, d = x.shape
    depth = w_ada.shape[0]
    n_tok = bsz * seq
    row2 = lambda a: a.reshape(1, -1)

    cos_b, sin_b, seg_cos, seg_nsin = (jnp.asarray(t).astype(BF16) for t in _dft_tables(_tile(seq, HYENA_BLOCK)))
    z, z_rev = _filter_features(seq)
    min_decay = math.log(DECAY_TARGET) / SLOW_DECAY_PCT
    max_decay = math.log(DECAY_TARGET) / FAST_DECAY_PCT
    absdelta = jnp.abs(jnp.linspace(min_decay, max_decay, HYENA_WIDTH, dtype=F32))[None, :]
    kpos_a, kpos_b, qpos_a, qpos_b = (t.astype(BF16) for t in _alibi_tables(seq))

    n_blocks = -(-n_tok * TOP_K // EXPERT_ROWS) + N_EXPERTS
    n_rows = n_blocks * EXPERT_ROWS

    assert depth == 1, "LAMBDA_INIT is the first layer's"
    for l in range(depth):
        mod3 = _adaln_mod(c, w_ada[l], row2(b_ada[l])).reshape(bsz, 6, d)

        u_hy, q, k_a, k_b, v_ones, sg = _in_proj(x, mod3, row2(norm_mix_g[l]), w_in[l].astype(BF16), row2(b_in[l]),
                                                 kpos_a, kpos_b)

        w1p = jnp.pad(filt_w1[l], ((0, 128 - filt_w1.shape[1]), (0, 0)))
        kr, kq = _hyena_filters(z, z_rev, w1p, row2(filt_b1[l]), filt_w2[l], row2(filt_b2[l]), filt_w3[l],
                                row2(filt_b3[l]), filt_w4[l], row2(filt_freq[l]), absdelta, seg_cos, seg_nsin)
        y_hy = _hyena(u_hy, hy_conv_w[l], row2(hy_conv_b[l]), row2(hy_d_skip[l]), cos_b, sin_b, kr, kq)

        y_at = _diff_attention(q, k_a, k_b, v_ones, qpos_a, qpos_b, lambda_qk[l], row2(attn_subln_g[l]))

        x1, h2, logits_t = _merge(
            y_hy, y_at, sg, x, mod3, w_hy_out[l].astype(BF16), w_attn_out[l].astype(BF16),
            w_mix_out[l].astype(BF16), row2(norm_ffn_g[l]), w_router[l].T, b_router[l].reshape(-1, 1))

        idx_t, gate_t, rank_t, counts = _route(logits_t)
        counts = counts[:, 0].astype(I32)
        padded = ((counts + EXPERT_ROWS - 1) // EXPERT_ROWS) * EXPERT_ROWS
        pad_end = jnp.cumsum(padded)
        pad_start = pad_end - padded
        dest_t = _dest_rows(idx_t, rank_t, jnp.broadcast_to(pad_start[:, None], (N_EXPERTS, 128)))
        dest_flat = dest_t[:TOP_K].T.reshape(-1)
        block_start = jnp.arange(n_blocks, dtype=I32) * EXPERT_ROWS
        block_expert = jnp.sum((block_start[:, None] >= pad_end[None, :]).astype(I32), axis=1)
        block_expert = jnp.minimum(block_expert, N_EXPERTS - 1)
        n_used = (pad_end[-1:] // EXPERT_ROWS).astype(I32)

        xs = _dispatch(dest_flat, pad_end.astype(I32), h2.reshape(n_tok, d), n_rows)
        ys = _experts(block_expert, n_used, xs, w_gate_up[l], b_gate_up[l][:, None, :],
                      w_down[l], b_down[l][:, None, :])
        x = _combine(dest_flat, gate_t, x1.reshape(n_tok, d), mod3, row2(final_norm_g), ys, seq).reshape(bsz, seq, d)
    return x
```

```python
import functools
import math

import numpy as np
import jax
import jax.numpy as jnp
from jax import lax
from jax.experimental import pallas as pl
from jax.experimental.pallas import tpu as pltpu

F32 = jnp.float32
BF16 = jnp.bfloat16
I32 = jnp.int32

EPS = 1e-5
HYENA_WIDTH = 512
FILTER_BANDS = 8
DECAY_TARGET = 1e-2
FAST_DECAY_PCT = 0.3
SLOW_DECAY_PCT = 1.5
ATTN_HEADS = 4
ATTN_HEAD_DIM = 64
ATTN_V_DIM = 2 * ATTN_HEAD_DIM
ATTN_WIDTH = ATTN_HEADS * ATTN_V_DIM
N_EXPERTS = 32
TOP_K = 4
SWIGLU_LIMIT = 7.0
SWIGLU_ALPHA = 1.702
LAMBDA_INIT = 0.8 - 0.6 * math.exp(-0.3 * 0)
ALIBI_SPLIT_LOG2 = 6

IN_PROJ_ROWS = 512
ATTN_Q_ROWS = 512
ATTN_BATCH_PER_STEP = 4
HYENA_BLOCK = 512
MERGE_ROWS = 512
ROUTE_TOKENS = 512
DEST_TOKENS = 2048
DISPATCH_TOKENS = 256
EXPERT_ROWS = 512
COMBINE_TOKENS = 256
ROUTE_SUBLANES = 8
ROW_TILE_SUBLANES = 8
LANES = 128

VMEM_LIMIT_BYTES = 56 * 1024 * 1024


def _tile(n, t):
    t = min(n, t)
    assert n % t == 0, (n, t)
    return t


def _params(n_axes):
    return pltpu.CompilerParams(
        dimension_semantics=("arbitrary",) * n_axes, vmem_limit_bytes=VMEM_LIMIT_BYTES
    )


def _split_bf16(a):
    hi = a.astype(BF16)
    lo = (a - hi.astype(F32)).astype(BF16)
    return hi, lo


_NN = (((1,), (0,)), ((), ()))
_NT = (((1,), (1,)), ((), ()))


def _dot3(a, b, dims=_NN):
    ah, al = _split_bf16(a)
    bh, bl = _split_bf16(b)
    d = lambda x, y: lax.dot_general(x, y, dims, preferred_element_type=F32)
    return d(ah, bh) + d(ah, bl) + d(al, bh)


def _dot(a, b):
    return jnp.dot(a, b, preferred_element_type=F32)


def _rms(x):
    return x * lax.rsqrt(jnp.mean(x * x, axis=-1, keepdims=True) + EPS)


def _mod_kernel(c_ref, w_ref, b_ref, o_ref):
    c = c_ref[...]
    o_ref[...] = _dot3(c * jax.nn.sigmoid(c), w_ref[...]) + b_ref[...]


def _adaln_mod(c, w, b):
    bsz, d = c.shape
    n = w.shape[1]
    tn = _tile(n, 1536)
    return pl.pallas_call(
        _mod_kernel,
        grid=(n // tn,),
        in_specs=[
            pl.BlockSpec((bsz, d), lambda j: (0, 0)),
            pl.BlockSpec((d, tn), lambda j: (0, j)),
            pl.BlockSpec((1, tn), lambda j: (0, j)),
        ],
        out_specs=pl.BlockSpec((bsz, tn), lambda j: (0, j)),
        out_shape=jax.ShapeDtypeStruct((bsz, n), F32),
        compiler_params=_params(1),
        name="adaln_mod",
    )(c, w, b)


def _alibi_tables(seq):
    slopes = 2.0 ** (-8.0 * jnp.arange(1, ATTN_HEADS + 1, dtype=F32) / ATTN_HEADS)
    pos = jnp.arange(seq, dtype=I32)
    hi = (jnp.right_shift(pos, ALIBI_SPLIT_LOG2) << ALIBI_SPLIT_LOG2).astype(F32)[:, None] * slopes[None, :]
    lo = (pos & ((1 << ALIBI_SPLIT_LOG2) - 1)).astype(F32)[:, None] * slopes[None, :]
    one = jnp.ones_like(hi)
    key4 = jnp.stack([hi, lo, one, one], axis=-1)
    qry4 = jnp.stack([one, one, -hi, -lo], axis=-1)

    def place(vals, first_lane):
        tab = jnp.zeros((seq, ATTN_HEADS, ATTN_V_DIM), F32).at[:, :, first_lane:first_lane + 4].set(vals)
        return tab.reshape(seq, ATTN_WIDTH)

    half = ATTN_HEAD_DIM
    return place(key4, half), place(key4, 0), place(qry4, half), place(qry4, 0)


def _in_proj_kernel(x_ref, mod_ref, g_ref, w_ref, b_ref, kpa_ref, kpb_ref,
                    uhy_ref, q_ref, ka_ref, kb_ref, v_ref, sg_ref, *, cols):
    x = x_ref[0]
    h = _rms(x) * g_ref[...]
    h = h * (1.0 + mod_ref[0, 1:2, :]) + mod_ref[0, 0:1, :]
    hb = h.astype(BF16)

    def proj(lo, hi):
        return _dot(hb, w_ref[:, lo:hi]) + b_ref[:, lo:hi]

    c0, c1, c2, c3, c4 = cols
    uhy_ref[0] = proj(0, c0).astype(BF16)
    q_ref[0] = (proj(c0, c1) * (1.0 / math.sqrt(ATTN_HEAD_DIM))).astype(BF16)
    k = proj(c1, c2).astype(BF16)
    first_half = (lax.broadcasted_iota(I32, (1, ATTN_WIDTH), 1) % ATTN_V_DIM) < ATTN_HEAD_DIM
    ka_ref[0] = jnp.where(first_half, k, kpa_ref[...])
    kb_ref[0] = jnp.where(first_half, kpb_ref[...], k)
    v = proj(c2, c3).astype(BF16)
    ones_col = jnp.where(lax.broadcasted_iota(I32, (v.shape[0], ATTN_V_DIM), 1) == 0, 1.0, 0.0).astype(BF16)
    pieces = []
    for hd in range(ATTN_HEADS):
        pieces += [v[:, hd * ATTN_V_DIM:(hd + 1) * ATTN_V_DIM], ones_col]
    v_ref[0] = jnp.concatenate(pieces, axis=1)
    sg_ref[0] = jax.nn.sigmoid(proj(c3, c4)).astype(BF16)


def _in_proj(x, mod3, g, w_bf16, b, kpos_a, kpos_b):
    bsz, seq, d = x.shape
    n = w_bf16.shape[1]
    hw3 = 3 * HYENA_WIDTH
    cols = (hw3, hw3 + ATTN_WIDTH, hw3 + 2 * ATTN_WIDTH, hw3 + 3 * ATTN_WIDTH, n)
    tm = _tile(seq, IN_PROJ_ROWS)
    row = lambda width: pl.BlockSpec((1, tm, width), lambda bi, i: (bi, i, 0))
    out = lambda width: jax.ShapeDtypeStruct((bsz, seq, width), BF16)
    pos = pl.BlockSpec((tm, ATTN_WIDTH), lambda bi, i: (i, 0))
    widths = (hw3, ATTN_WIDTH, ATTN_WIDTH, ATTN_WIDTH, 2 * ATTN_WIDTH, 2 * d)
    return pl.pallas_call(
        functools.partial(_in_proj_kernel, cols=cols),
        grid=(bsz, seq // tm),
        in_specs=[
            row(d),
            pl.BlockSpec((1, 6, d), lambda bi, i: (bi, 0, 0)),
            pl.BlockSpec((1, d), lambda bi, i: (0, 0)),
            pl.BlockSpec((d, n), lambda bi, i: (0, 0)),
            pl.BlockSpec((1, n), lambda bi, i: (0, 0)),
            pos, pos,
        ],
        out_specs=[row(w) for w in widths],
        out_shape=[out(w) for w in widths],
        compiler_params=_params(2),
        name="in_proj",
    )(x, mod3, g, w_bf16, b, kpos_a, kpos_b)


@functools.lru_cache(maxsize=None)
def _dft_tables(p):
    n_fft = 2 * p
    f = np.arange(p, dtype=np.int64)[:, None]
    ang = lambda n: 2.0 * np.pi * ((f * n[None, :]) % n_fft).astype(np.float64) / n_fft
    n_blk = np.arange(p, dtype=np.int64)
    n_seg = np.arange(-p, p, dtype=np.int64)
    keep = (n_seg != -p)[None, :]
    f32 = lambda a: a.astype(np.float32)
    return (f32(np.cos(ang(n_blk))), f32(np.sin(ang(n_blk))),
            f32(np.cos(ang(n_seg)) * keep), f32(-np.sin(ang(n_seg)) * keep))


def _filter_features(seq):
    t = jnp.linspace(0.0, 1.0, seq, dtype=F32)[:, None]
    w = 2.0 * math.pi * jnp.arange(seq, dtype=F32)[:, None] / seq
    fr = jnp.linspace(1e-4, FILTER_BANDS - 1, FILTER_BANDS, dtype=F32)[None, :]
    z = jnp.concatenate([t, jnp.cos(fr * w), -jnp.sin(fr * w)], axis=-1)
    z = jnp.pad(z, ((0, 0), (0, 128 - z.shape[1])))
    z_rev = jnp.concatenate([z[:1], z[:0:-1]], axis=0)
    return z, z_rev


def _alternating(rows):
    return jnp.where((rows & 1) == 0, 1.0, -1.0).astype(F32)


def _filter_kernel(z_ref, zrev_ref, w1_ref, b1_ref, w2_ref, b2_ref, w3_ref, b3_ref, w4_ref, fq_ref, ad_ref,
                   segcos_ref, segsin_ref, kr_ref, kq_ref, taps_ref):
    seq = z_ref.shape[0]
    width = ad_ref.shape[1]
    p = segcos_ref.shape[0]
    fq = fq_ref[...]

    def taps(feat, lo):
        h = jnp.sin(fq * (_dot3(feat, w1_ref[...]) + b1_ref[...]))
        h = jnp.sin(fq * (_dot3(h, w2_ref[...]) + b2_ref[...]))
        h = jnp.sin(fq * (_dot3(h, w3_ref[...]) + b3_ref[...]))
        return _dot3(h, w4_ref[:, lo:lo + width]) * jnp.exp(-feat[:, 0:1] * ad_ref[...])

    lag_is_minus_l = lax.broadcasted_iota(I32, (seq, 1), 0) == 0
    taps_ref[0:seq, :] = jnp.where(lag_is_minus_l, 0.0, taps(zrev_ref[...], width))
    taps_ref[seq:2 * seq, :] = taps(z_ref[...], 0)

    rows = lax.broadcasted_iota(I32, (p, 1), 0)
    first = rows == 0
    scale = jnp.where(first, 0.5 / p, 1.0 / p)
    lag = lax.broadcasted_iota(I32, (2 * p, 1), 0)
    alt = jnp.where(lag == 0, 0.0, _alternating(lag))
    n_seg = kr_ref.shape[0]
    for s in range(n_seg):
        d = s - n_seg // 2
        seg = taps_ref[(d - 1) * p + seq:(d + 1) * p + seq, :]
        seg_b = seg.astype(BF16)
        kr = _dot(segcos_ref[...], seg_b)
        kq = jnp.where(first, jnp.sum(alt * seg, axis=0, keepdims=True), _dot(segsin_ref[...], seg_b))
        kr_ref[s] = kr * scale
        kq_ref[s] = kq * scale


def _hyena_filters(z, z_rev, w1, b1, w2, b2, w3, b3, w4, fq, absdelta, seg_cos, seg_nsin):
    seq = z.shape[0]
    width = absdelta.shape[1]
    p = seg_cos.shape[0]
    n_seg = 2 * (seq // p) - 1
    args = (z, z_rev, w1, b1, w2, b2, w3, b3, w4, fq, absdelta, seg_cos, seg_nsin)
    full = lambda a: pl.BlockSpec(a.shape, lambda i: (0,) * a.ndim)
    return pl.pallas_call(
        _filter_kernel,
        grid=(1,),
        in_specs=[full(a) for a in args],
        out_specs=[pl.BlockSpec((n_seg, p, width), lambda i: (0, 0, 0))] * 2,
        out_shape=[jax.ShapeDtypeStruct((n_seg, p, width), F32)] * 2,
        scratch_shapes=[pltpu.VMEM((2 * seq, width), F32)],
        compiler_params=_params(1),
        name="hyena_filters",
    )(*args)


def _hyena_kernel(x0_ref, x1_ref, v_ref, w0_ref, w1_ref, wv_ref, b0_ref, b1_ref, bv_ref, dskip_ref,
                  cos_ref, sin_ref, kr_ref, kq_ref, o_ref):
    seq = x0_ref.shape[1]
    rows = lax.broadcasted_iota(I32, (seq, 1), 0)
    first = rows == 0
    last = rows == seq - 1
    alt = _alternating(rows)

    def short_conv(u_ref, w_ref, b_ref):
        u = u_ref[0].astype(F32)
        prev = jnp.where(first, 0.0, pltpu.roll(u, 1, 0))
        nxt = jnp.where(last, 0.0, pltpu.roll(u, seq - 1, 0))
        w = w_ref[...]
        return prev * w[0:1] + u * w[1:2] + nxt * w[2:3] + b_ref[...]

    x0 = short_conv(x0_ref, w0_ref, b0_ref)
    x1 = short_conv(x1_ref, w1_ref, b1_ref)
    v = short_conv(v_ref, wv_ref, bv_ref) * x1
    vb = v.astype(BF16)
    cos_m = cos_ref[...]
    sin_m = sin_ref[...]
    p = cos_m.shape[0]
    nb = seq // p
    first_p = first[:p]
    alt_p = alt[:p]
    vr, vq = [], []
    for j in range(nb):
        blk = slice(j * p, (j + 1) * p)
        vr.append(_dot(cos_m, vb[blk]))
        vq.append(jnp.where(first_p, jnp.sum(alt_p * v[blk], axis=0, keepdims=True), _dot(sin_m, vb[blk])))
    y_blocks = []
    for i in range(nb):
        yr = yq = dc = nyq = None
        for j in range(nb):
            kr = kr_ref[i - j + nb - 1]
            kq = kq_ref[i - j + nb - 1]
            terms = (vr[j] * kr + vq[j] * kq, vq[j] * kr - vr[j] * kq, vr[j][0:1] * kr[0:1], vq[j][0:1] * kq[0:1])
            yr, yq, dc, nyq = terms if j == 0 else (yr + terms[0], yq + terms[1], dc + terms[2], nyq + terms[3])
        yr = jnp.where(first_p, dc, yr)
        yq = jnp.where(first_p, nyq, yq)
        y_blocks.append(_dot(cos_m, yr.astype(BF16)) + _dot(sin_m, yq.astype(BF16)) + alt_p * nyq)
    y = jnp.concatenate(y_blocks, axis=0)
    o_ref[0] = ((y + v * dskip_ref[...]) * x0).astype(BF16)


def _hyena(u_hy, conv_w, conv_b, d_skip, cos_b, sin_b, kr, kq):
    bsz, seq, _ = u_hy.shape
    width = d_skip.shape[1]
    p = cos_b.shape[0]
    cw = _tile(width, 256)
    nc = width // cw
    u_spec = lambda part: pl.BlockSpec((1, seq, cw), lambda c, b: (b, 0, part * nc + c))
    w_spec = lambda part: pl.BlockSpec((3, cw), lambda c, b: (0, part * nc + c))
    b_spec = lambda part: pl.BlockSpec((1, cw), lambda c, b: (0, part * nc + c))
    full = pl.BlockSpec((p, p), lambda c, b: (0, 0))
    k_spec = pl.BlockSpec((kr.shape[0], p, cw), lambda c, b: (0, 0, c))
    return pl.pallas_call(
        _hyena_kernel,
        grid=(nc, bsz),
        in_specs=[u_spec(0), u_spec(1), u_spec(2), w_spec(0), w_spec(1), w_spec(2),
                  b_spec(0), b_spec(1), b_spec(2), pl.BlockSpec((1, cw), lambda c, b: (0, c)),
                  full, full, k_spec, k_spec],
        out_specs=pl.BlockSpec((1, seq, cw), lambda c, b: (b, 0, c)),
        out_shape=jax.ShapeDtypeStruct((bsz, seq, width), BF16),
        compiler_params=_params(2),
        name="hyena",
    )(u_hy, u_hy, u_hy, conv_w, conv_w, conv_w, conv_b, conv_b, conv_b, d_skip, cos_b, sin_b, kr, kq)


def _attn_kernel(q_ref, ka_ref, kb_ref, v_ref, qpa_ref, qpb_ref, lq_ref, g_ref, o_ref, s_ref, bias_ref):
    head = pl.program_id(0)
    qi = pl.program_id(1)
    tq = q_ref.shape[1]
    seq = ka_ref.shape[1]
    nk = seq // tq

    @pl.when(pl.program_id(2) == 0)
    def _():
        slope = jnp.where(head == 0, 0.25, jnp.where(head == 1, 0.0625, jnp.where(head == 2, 0.015625, 0.00390625)))
        r = lax.broadcasted_iota(I32, (tq, tq), 0)
        c = lax.broadcasted_iota(I32, (tq, tq), 1)
        bias_ref[...] = jnp.abs(r - c).astype(F32) * (-slope)

    first_half = lax.broadcasted_iota(I32, (1, ATTN_V_DIM), 1) < ATTN_HEAD_DIM
    maps = ((ka_ref, first_half, qpa_ref[...]), (kb_ref, ~first_half, qpb_ref[...]))
    starts = [pl.multiple_of(lax.rem(qi + r, nk) * tq, tq) for r in range(nk)]
    lq = lq_ref[...]
    lam = (jnp.exp(jnp.sum(lq[0:1] * lq[1:2], axis=-1, keepdims=True))
           - jnp.exp(jnp.sum(lq[2:3] * lq[3:4], axis=-1, keepdims=True)) + LAMBDA_INIT)

    for item in range(q_ref.shape[0]):
        q = q_ref[item]
        zero = jnp.zeros_like(q)
        sset = item % 2
        row_max = []
        for m, (k_ref, own, q_pos) in enumerate(maps):
            q_diag = jnp.where(own, q, zero)
            q_left = jnp.where(own, q, q_pos)
            q_right = jnp.where(own, q, -q_pos)
            for r in range(nk):
                kb = k_ref[item, pl.ds(starts[r], tq), :]
                if r == 0:
                    s = lax.dot_general(q_diag, kb, _NT, preferred_element_type=F32) + bias_ref[...]
                else:
                    q_side = jnp.where(lax.rem(qi + r, nk) < qi, q_left, q_right)
                    s = lax.dot_general(q_side, kb, _NT, preferred_element_type=F32)
                s_ref[sset, m, r] = s
                blk_max = jnp.max(s, axis=-1, keepdims=True)
                if r == 0:
                    row_max.append(blk_max)
                else:
                    row_max[m] = jnp.maximum(row_max[m], blk_max)

        outs = []
        for m in range(2):
            acc = jnp.zeros((tq, 2 * ATTN_V_DIM), F32)
            for r in range(nk):
                p = jnp.exp(s_ref[sset, m, r] - row_max[m]).astype(BF16)
                acc = acc + _dot(p, v_ref[item, pl.ds(starts[r], tq), :])
            outs.append(acc[:, :ATTN_V_DIM] / acc[:, ATTN_V_DIM:ATTN_V_DIM + 1])

        o = outs[0] - lam * outs[1]
        o_ref[item] = (_rms(o) * g_ref[...] * (1.0 - LAMBDA_INIT)).astype(BF16)


def _diff_attention(q, k_a, k_b, v_ones, qpos_a, qpos_b, lambda_qk, subln_g):
    bsz, seq, _ = q.shape
    tq = _tile(seq, ATTN_Q_ROWS)
    nbat = _tile(bsz, ATTN_BATCH_PER_STEP)
    k_spec = pl.BlockSpec((nbat, seq, ATTN_V_DIM), lambda h, i, b: (b, 0, h))
    v_spec = pl.BlockSpec((nbat, seq, 2 * ATTN_V_DIM), lambda h, i, b: (b, 0, h))
    q_spec = pl.BlockSpec((nbat, tq, ATTN_V_DIM), lambda h, i, b: (b, i, h))
    qpos_spec = pl.BlockSpec((tq, ATTN_V_DIM), lambda h, i, b: (i, h))
    return pl.pallas_call(
        _attn_kernel,
        grid=(ATTN_HEADS, seq // tq, bsz // nbat),
        in_specs=[q_spec, k_spec, k_spec, v_spec, qpos_spec, qpos_spec,
                  pl.BlockSpec(lambda_qk.shape, lambda h, i, b: (0, 0)),
                  pl.BlockSpec(subln_g.shape, lambda h, i, b: (0, 0))],
        out_specs=q_spec,
        out_shape=jax.ShapeDtypeStruct((bsz, seq, ATTN_WIDTH), BF16),
        scratch_shapes=[pltpu.VMEM((2, 2, seq // tq, tq, tq), F32), pltpu.VMEM((tq, tq), F32)],
        compiler_params=_params(3),
        name="diff_attention",
    )(q, k_a, k_b, v_ones, qpos_a, qpos_b, lambda_qk, subln_g)


def _merge_kernel(yhy_ref, yat_ref, sg_ref, x_ref, mod_ref, why_ref, wat_ref, wmix_ref, g_ref,
                  wr_ref, br_ref, x1_ref, h2_ref, lg_ref):
    d = x_ref.shape[2]
    sg = sg_ref[0]
    merged = (sg[:, :d].astype(F32) * _dot(yhy_ref[0], why_ref[...])
              + sg[:, d:].astype(F32) * _dot(yat_ref[0], wat_ref[...]))
    x1 = x_ref[0] + mod_ref[0, 2:3, :] * _dot(merged.astype(BF16), wmix_ref[...])
    x1_ref[0] = x1
    h2 = _rms(x1) * g_ref[...]
    h2 = h2 * (1.0 + mod_ref[0, 4:5, :]) + mod_ref[0, 3:4, :]
    h2_ref[0] = h2
    lg_ref[0] = _dot3(wr_ref[...], h2, _NT) + br_ref[...]


def _merge(y_hy, y_at, sg, x, mod3, w_hy, w_at, w_mix, g, w_router, b_router):
    bsz, seq, d = x.shape
    tm = _tile(seq, MERGE_ROWS)
    row = lambda width: pl.BlockSpec((1, tm, width), lambda b, i: (b, i, 0))
    full = lambda a: pl.BlockSpec(a.shape, lambda b, i: (0,) * a.ndim)
    return pl.pallas_call(
        _merge_kernel,
        grid=(bsz, seq // tm),
        in_specs=[row(y_hy.shape[2]), row(y_at.shape[2]), row(2 * d), row(d),
                  pl.BlockSpec((1, 6, d), lambda b, i: (b, 0, 0)),
                  full(w_hy), full(w_at), full(w_mix), full(g), full(w_router), full(b_router)],
        out_specs=[row(d), row(d), pl.BlockSpec((1, N_EXPERTS, tm), lambda b, i: (b, 0, i))],
        out_shape=[jax.ShapeDtypeStruct((bsz, seq, d), F32), jax.ShapeDtypeStruct((bsz, seq, d), F32),
                   jax.ShapeDtypeStruct((bsz, N_EXPERTS, seq), F32)],
        compiler_params=_params(2),
        name="merge_router",
    )(y_hy, y_at, sg, x, mod3, w_hy, w_at, w_mix, g, w_router, b_router)


def _route_kernel(lg_ref, idx_ref, gate_ref, rank_ref, cnt_ref, carry_ref):
    step = pl.program_id(0) * pl.num_programs(1) + pl.program_id(1)

    @pl.when(step == 0)
    def _():
        carry_ref[...] = jnp.zeros_like(carry_ref)

    work = lg_ref[0]
    n_exp, tr = work.shape
    e_iota = lax.broadcasted_iota(I32, (n_exp, tr), 0)
    vals, idxs = [], []
    for _ in range(TOP_K):
        m = jnp.max(work, axis=0, keepdims=True)
        ik = jnp.min(jnp.where(work == m, e_iota, n_exp), axis=0, keepdims=True)
        vals.append(m)
        idxs.append(ik)
        work = jnp.where(e_iota == ik, -jnp.inf, work)
    ex = [jnp.exp(val - vals[0]) for val in vals]
    denom = ex[0] + ex[1] + ex[2] + ex[3]
    sel = jnp.zeros((n_exp, tr), F32)
    for ik in idxs:
        sel = sel + jnp.where(e_iota == ik, 1.0, 0.0)
    upper = jnp.where(lax.broadcasted_iota(I32, (tr, tr), 0) < lax.broadcasted_iota(I32, (tr, tr), 1), 1.0, 0.0)
    rank_all = _dot(sel.astype(BF16), upper.astype(BF16)) + carry_ref[:, 0:1]
    pad = ROUTE_SUBLANES - TOP_K
    ranks = [jnp.sum(jnp.where(e_iota == ik, rank_all, 0.0), axis=0, keepdims=True) for ik in idxs]
    idx_ref[...] = jnp.concatenate(idxs + [jnp.zeros((pad, tr), I32)], axis=0)
    gate_ref[...] = jnp.concatenate([e / denom for e in ex] + [jnp.zeros((pad, tr), F32)], axis=0)
    rank_ref[...] = jnp.concatenate(ranks + [jnp.zeros((pad, tr), F32)], axis=0).astype(I32)
    carry_ref[...] = carry_ref[...] + jnp.sum(sel, axis=1, keepdims=True)
    cnt_ref[...] = carry_ref[...]


def _route(logits_t):
    bsz, n_exp, seq = logits_t.shape
    tr = _tile(seq, ROUTE_TOKENS)
    nt = seq // tr
    tok = pl.BlockSpec((ROUTE_SUBLANES, tr), lambda b, i: (0, b * nt + i))
    shape = lambda dt: jax.ShapeDtypeStruct((ROUTE_SUBLANES, bsz * seq), dt)
    return pl.pallas_call(
        _route_kernel,
        grid=(bsz, nt),
        in_specs=[pl.BlockSpec((1, n_exp, tr), lambda b, i: (b, 0, i))],
        out_specs=[tok, tok, tok, pl.BlockSpec((n_exp, 128), lambda b, i: (0, 0))],
        out_shape=[shape(I32), shape(F32), shape(I32), jax.ShapeDtypeStruct((n_exp, 128), F32)],
        scratch_shapes=[pltpu.VMEM((n_exp, 128), F32)],
        compiler_params=_params(2),
        name="route_topk",
    )(logits_t)


def _dest_kernel(idx_ref, rank_ref, start_ref, dest_ref):
    idx = idx_ref[...]
    n_exp = start_ref.shape[0]
    tr = idx.shape[1]
    e_iota = lax.broadcasted_iota(I32, (n_exp, tr), 0)
    start = start_ref[:, 0:1]
    rows = [jnp.sum(jnp.where(e_iota == idx[k:k + 1, :], start, 0), axis=0, keepdims=True) for k in range(TOP_K)]
    rows.append(jnp.zeros((ROUTE_SUBLANES - TOP_K, tr), I32))
    dest_ref[...] = rank_ref[...] + jnp.concatenate(rows, axis=0)


def _dest_rows(idx_t, rank_t, pad_start):
    n_tok = idx_t.shape[1]
    tr = _tile(n_tok, DEST_TOKENS)
    tok = pl.BlockSpec((ROUTE_SUBLANES, tr), lambda i: (0, i))
    return pl.pallas_call(
        _dest_kernel,
        grid=(n_tok // tr,),
        in_specs=[tok, tok, pl.BlockSpec(pad_start.shape, lambda i: (0, 0))],
        out_specs=tok,
        out_shape=jax.ShapeDtypeStruct(idx_t.shape, I32),
        compiler_params=_params(1),
        name="dest_rows",
    )(idx_t, rank_t, pad_start)


def _relayout_copies(flat_ref, tiled_ref, sem, to_tiles):
    copies = []
    for s in range(ROW_TILE_SUBLANES):
        flat = flat_ref.at[:, pl.ds(s * LANES, LANES)]
        tiled = tiled_ref.at[:, s, :]
        copies.append(pltpu.make_async_copy(flat, tiled, sem) if to_tiles else pltpu.make_async_copy(tiled, flat, sem))
    return copies


def _dispatch_kernel(dest_ref, pend_ref, h_ref, xs_ref, rows_ref, zeros_ref, load_sems, row_sems):
    step = pl.program_id(0)
    last = pl.num_programs(0) - 1
    tt = rows_ref.shape[1]
    bm = zeros_ref.shape[0]
    slot = lax.rem(step, 3)
    slot_next = lax.rem(step + 1, 3)
    slot_prev = lax.rem(step + 2, 3)

    def load(tile, sl):
        return _relayout_copies(h_ref.at[pl.ds(tile * tt, tt)], rows_ref.at[sl], load_sems.at[sl], True)

    @pl.when(step == 0)
    def _():
        for c in load(0, 0):
            c.start()
        zeros_ref[...] = jnp.zeros_like(zeros_ref)

        def zero_copy(start):
            return pltpu.make_async_copy(zeros_ref, xs_ref.at[pl.ds(start, bm)], row_sems.at[0])

        for e in range(N_EXPERTS):
            zero_copy(jnp.maximum(pend_ref[e] - bm, 0)).start()
        for e in range(N_EXPERTS):
            zero_copy(0).wait()

        def zero_tail(j, carry):
            zero_copy(j * bm).start()
            zero_copy(0).wait()
            return carry

        lax.fori_loop(pend_ref[N_EXPERTS - 1] // bm, xs_ref.shape[0] // bm, zero_tail, 0)

    @pl.when(step < last)
    def _():
        for c in load(step + 1, slot_next):
            c.start()

    for c in load(step, slot):
        c.wait()

    def issue(t, carry):
        for k in range(TOP_K):
            pltpu.make_async_copy(rows_ref.at[slot, t], xs_ref.at[dest_ref[t * TOP_K + k]],
                                  row_sems.at[slot]).start(priority=k % 2)
        return carry

    lax.fori_loop(0, tt, issue, 0, unroll=4)

    def drain(sl):
        def body(t, carry):
            for _ in range(TOP_K):
                pltpu.make_async_copy(rows_ref.at[sl, 0], xs_ref.at[0], row_sems.at[sl]).wait()
            return carry

        lax.fori_loop(0, tt, body, 0, unroll=4)

    @pl.when(step > 0)
    def _():
        drain(slot_prev)

    @pl.when(step == last)
    def _():
        drain(slot)


def _dispatch(dest_flat, pad_end, h2, n_rows):
    n_tok, d = h2.shape
    assert d == ROW_TILE_SUBLANES * LANES
    tt = _tile(n_tok, DISPATCH_TOKENS)
    tile = (ROW_TILE_SUBLANES, LANES)
    return pl.pallas_call(
        _dispatch_kernel,
        grid=(n_tok // tt,),
        in_specs=[pl.BlockSpec((tt * TOP_K,), lambda i: (i,), memory_space=pltpu.SMEM),
                  pl.BlockSpec(memory_space=pltpu.SMEM),
                  pl.BlockSpec(memory_space=pl.ANY)],
        out_specs=pl.BlockSpec(memory_space=pl.ANY),
        out_shape=jax.ShapeDtypeStruct((n_rows,) + tile, F32),
        scratch_shapes=[pltpu.VMEM((3, tt) + tile, F32), pltpu.VMEM((EXPERT_ROWS,) + tile, F32),
                        pltpu.SemaphoreType.DMA((3,)), pltpu.SemaphoreType.DMA((3,))],
        compiler_params=_params(1),
        name="moe_dispatch",
    )(dest_flat, pad_end, h2)


def _expert_kernel(be_ref, nu_ref, xs_ref, wgu_ref, bgu_ref, wd_ref, bd_ref, ys_ref,
                   x_buf, y_buf, wgu_bf16, wd_bf16, in_sems, out_sems):
    step = pl.program_id(0)
    n_used = nu_ref[0]
    bm = x_buf.shape[1]
    de = wd_ref.shape[1]
    slot = step % 2

    def load(block, sl):
        return _relayout_copies(x_buf.at[sl], xs_ref.at[pl.ds(block * bm, bm)], in_sems.at[sl], False)

    def store(block, sl):
        return _relayout_copies(y_buf.at[sl], ys_ref.at[pl.ds(block * bm, bm)], out_sems.at[sl], True)

    @pl.when(step < n_used)
    def _():
        @pl.when(step == 0)
        def _():
            for c in load(0, 0):
                c.start()

        @pl.when(step + 1 < n_used)
        def _():
            for c in load(step + 1, 1 - slot):
                c.start()

        @pl.when((step == 0) | (be_ref[step] != be_ref[jnp.maximum(step - 1, 0)]))
        def _():
            wgu_bf16[...] = wgu_ref[0].astype(BF16)
            wd_bf16[...] = wd_ref[0].astype(BF16)

        for c in load(step, slot):
            c.wait()
        gu = _dot(x_buf[slot].astype(BF16), wgu_bf16[...]) + bgu_ref[0]
        gate = jnp.minimum(gu[:, :de], SWIGLU_LIMIT)
        up = jnp.clip(gu[:, de:], -SWIGLU_LIMIT, SWIGLU_LIMIT)
        glu = gate * jax.nn.sigmoid(SWIGLU_ALPHA * gate)
        y = _dot(((up + 1.0) * glu).astype(BF16), wd_bf16[...]) + bd_ref[0]

        @pl.when(step >= 2)
        def _():
            for c in store(step - 2, slot):
                c.wait()

        y_buf[slot] = y
        for c in store(step, slot):
            c.start()

        @pl.when(step == n_used - 1)
        def _():
            @pl.when(step >= 1)
            def _():
                for c in store(step - 1, 1 - slot):
                    c.wait()

            for c in store(step, slot):
                c.wait()

    @pl.when(step >= n_used)
    def _():
        y_buf[slot] = jnp.zeros(y_buf.shape[1:], F32)
        for c in store(step, slot):
            c.start()
        for c in store(step, slot):
            c.wait()


def _experts(block_expert, n_used, xs, w_gu, b_gu, w_d, b_d):
    n_rows = xs.shape[0]
    n_blocks = n_rows // EXPERT_ROWS
    _, d, de2 = w_gu.shape
    de = w_d.shape[1]
    per_expert = lambda *dims: pl.BlockSpec((1,) + dims, lambda i, be, nu: (be[i], 0, 0))
    return pl.pallas_call(
        _expert_kernel,
        grid_spec=pltpu.PrefetchScalarGridSpec(
            num_scalar_prefetch=2,
            grid=(n_blocks,),
            in_specs=[pl.BlockSpec(memory_space=pl.ANY), per_expert(d, de2), per_expert(1, de2),
                      per_expert(de, d), per_expert(1, d)],
            out_specs=pl.BlockSpec(memory_space=pl.ANY),
            scratch_shapes=[pltpu.VMEM((2, EXPERT_ROWS, d), F32), pltpu.VMEM((2, EXPERT_ROWS, d), F32),
                            pltpu.VMEM((d, de2), BF16), pltpu.VMEM((de, d), BF16),
                            pltpu.SemaphoreType.DMA((2,)), pltpu.SemaphoreType.DMA((2,))],
        ),
        out_shape=jax.ShapeDtypeStruct(xs.shape, F32),
        compiler_params=_params(1),
        name="moe_experts",
    )(block_expert, n_used, xs, w_gu, b_gu, w_d, b_d)


def _combine_kernel(idx_ref, next_ref, gate_ref, mod_ref, g_ref, x1_ref, ys_ref, o_ref,
                    rows_buf, x1_buf, o_buf, gather_sems, load_sems, store_sems):
    step = pl.program_id(0)
    last = pl.num_programs(0) - 1
    tt = rows_buf.shape[2]
    d = x1_ref.shape[1]

    def load(tile, sl):
        return _relayout_copies(x1_ref.at[pl.ds(tile * tt, tt)], x1_buf.at[sl], load_sems.at[sl], True)

    def store(tile, sl):
        return _relayout_copies(o_ref.at[pl.ds(tile * tt, tt)], o_buf.at[sl], store_sems.at[sl], False)

    def drain_gathers(sl):
        def body(t, carry):
            for _ in range(TOP_K):
                pltpu.make_async_copy(ys_ref.at[0], rows_buf.at[sl, 0, 0], gather_sems.at[sl]).wait()
            return carry

        lax.fori_loop(0, tt, body, 0, unroll=4)

    @pl.when(step == 0)
    def _():
        def body(t, carry):
            for k in range(TOP_K):
                pltpu.make_async_copy(ys_ref.at[idx_ref[t * TOP_K + k]], rows_buf.at[0, k, t],
                                      gather_sems.at[0]).start(priority=k % 2)
            return carry

        lax.fori_loop(0, tt, body, 0, unroll=4)
        for c in load(0, 0):
            c.start()

    def body_for(slot):
        other = 1 - slot
        drain_gathers(slot)
        for c in load(step, slot):
            c.wait()

        @pl.when(step >= 2)
        def _():
            for c in store(step - 2, slot):
                c.wait()

        eye = lax.broadcasted_iota(I32, (tt, tt), 0) == lax.broadcasted_iota(I32, (tt, tt), 1)
        gates = gate_ref[...]
        tile_shape = (tt, ROW_TILE_SUBLANES, LANES)
        moe = None
        for k in range(TOP_K):
            col = jnp.sum(jnp.where(eye, gates[k:k + 1, :], 0.0), axis=1, keepdims=True)
            term = jnp.broadcast_to(col[:, :, None], tile_shape) * rows_buf[slot, k]
            moe = term if k == 0 else moe + term
        x2 = x1_buf[slot] + mod_ref[0, 5][None] * moe
        sq = jnp.sum(jnp.sum(x2 * x2, axis=2, keepdims=True), axis=1, keepdims=True)
        out = x2 * lax.rsqrt(sq / d + EPS) * g_ref[...][None]

        for t in range(tt):
            for k in range(TOP_K):
                pltpu.make_async_copy(ys_ref.at[next_ref[t * TOP_K + k]], rows_buf.at[other, k, t],
                                      gather_sems.at[other]).start(priority=k % 2)
        for c in load(jnp.minimum(step + 1, last), other):
            c.start()

        o_buf[slot] = out
        for c in store(step, slot):
            c.start()

        @pl.when(step == last)
        def _():
            drain_gathers(other)
            for c in load(last, other):
                c.wait()

            @pl.when(step >= 1)
            def _():
                for c in store(step - 1, other):
                    c.wait()

            for c in store(step, slot):
                c.wait()

    for parity in range(2):
        pl.when(step % 2 == parity)(functools.partial(body_for, parity))


def _combine(dest_flat, gate_t, x1, mod3, g, ys, seq):
    n_tok, d = x1.shape
    tt = _tile(seq, COMBINE_TOKENS)
    per_seq = seq // tt
    n_steps = n_tok // tt
    tile = (ROW_TILE_SUBLANES, LANES)
    idx_spec = lambda ahead: pl.BlockSpec((tt * TOP_K,), lambda i: (jnp.minimum(i + ahead, n_steps - 1),),
                                          memory_space=pltpu.SMEM)
    mod_tiles = mod3.reshape(mod3.shape[:2] + tile)
    return pl.pallas_call(
        _combine_kernel,
        grid=(n_steps,),
        in_specs=[idx_spec(0), idx_spec(1),
                  pl.BlockSpec((ROUTE_SUBLANES, tt), lambda i: (0, i)),
                  pl.BlockSpec((1, 6) + tile, lambda i: (i // per_seq, 0, 0, 0)),
                  pl.BlockSpec(tile, lambda i: (0, 0)),
                  pl.BlockSpec(memory_space=pl.ANY),
                  pl.BlockSpec(memory_space=pl.ANY)],
        out_specs=pl.BlockSpec(memory_space=pl.ANY),
        out_shape=jax.ShapeDtypeStruct((n_tok, d), F32),
        scratch_shapes=[pltpu.VMEM((2, TOP_K, tt) + tile, F32), pltpu.VMEM((2, tt) + tile, F32),
                        pltpu.VMEM((2, tt) + tile, F32),
                        pltpu.SemaphoreType.DMA((2,)), pltpu.SemaphoreType.DMA((2,)), pltpu.SemaphoreType.DMA((2,))],
        compiler_params=_params(1),
        name="moe_combine",
    )(dest_flat, dest_flat, gate_t, mod_tiles, g.reshape(tile), x1, ys)


def kernel(x, c, w_ada, b_ada, norm_mix_g, w_in, b_in, hy_conv_w, hy_conv_b, filt_w1, filt_b1, filt_w2, filt_b2, filt_w3, filt_b3, filt_w4, filt_freq, hy_d_skip, lambda_qk, attn_subln_g, w_hy_out, w_attn_out, w_mix_out, norm_ffn_g, w_router, b_router, w_gate_up, b_gate_up, w_down, b_down, final_norm_g):
    bsz, seq, d = x.shape
    depth = w_ada.shape[0]
    n_tok = bsz * seq
    row2 = lambda a: a.reshape(1, -1)

    cos_b, sin_b, seg_cos, seg_nsin = (jnp.asarray(t).astype(BF16) for t in _dft_tables(_tile(seq, HYENA_BLOCK)))
    z, z_rev = _filter_features(seq)
    min_decay = math.log(DECAY_TARGET) / SLOW_DECAY_PCT
    max_decay = math.log(DECAY_TARGET) / FAST_DECAY_PCT
    absdelta = jnp.abs(jnp.linspace(min_decay, max_decay, HYENA_WIDTH, dtype=F32))[None, :]
    kpos_a, kpos_b, qpos_a, qpos_b = (t.astype(BF16) for t in _alibi_tables(seq))

    n_blocks = -(-n_tok * TOP_K // EXPERT_ROWS) + N_EXPERTS
    n_rows = n_blocks * EXPERT_ROWS

    assert depth == 1, "LAMBDA_INIT is the first layer's"
    for l in range(depth):
        mod3 = _adaln_mod(c, w_ada[l], row2(b_ada[l])).reshape(bsz, 6, d)

        u_hy, q, k_a, k_b, v_ones, sg = _in_proj(x, mod3, row2(norm_mix_g[l]), w_in[l].astype(BF16), row2(b_in[l]),
                                                 kpos_a, kpos_b)

        w1p = jnp.pad(filt_w1[l], ((0, 128 - filt_w1.shape[1]), (0, 0)))
        kr, kq = _hyena_filters(z, z_rev, w1p, row2(filt_b1[l]), filt_w2[l], row2(filt_b2[l]), filt_w3[l],
                                row2(filt_b3[l]), filt_w4[l], row2(filt_freq[l]), absdelta, seg_cos, seg_nsin)
        y_hy = _hyena(u_hy, hy_conv_w[l], row2(hy_conv_b[l]), row2(hy_d_skip[l]), cos_b, sin_b, kr, kq)

        y_at = _diff_attention(q, k_a, k_b, v_ones, qpos_a, qpos_b, lambda_qk[l], row2(attn_subln_g[l]))

        x1, h2, logits_t = _merge(
            y_hy, y_at, sg, x, mod3, w_hy_out[l].astype(BF16), w_attn_out[l].astype(BF16),
            w_mix_out[l].astype(BF16), row2(norm_ffn_g[l]), w_router[l].T, b_router[l].reshape(-1, 1))

        idx_t, gate_t, rank_t, counts = _route(logits_t)
        counts = counts[:, 0].astype(I32)
        padded = ((counts + EXPERT_ROWS - 1) // EXPERT_ROWS) * EXPERT_ROWS
        pad_end = jnp.cumsum(padded)
        pad_start = pad_end - padded
        dest_t = _dest_rows(idx_t, rank_t, jnp.broadcast_to(pad_start[:, None], (N_EXPERTS, 128)))
        dest_flat = dest_t[:TOP_K].T.reshape(-1)
        block_start = jnp.arange(n_blocks, dtype=I32) * EXPERT_ROWS
        block_expert = jnp.sum((block_start[:, None] >= pad_end[None, :]).astype(I32), axis=1)
        block_expert = jnp.minimum(block_expert, N_EXPERTS - 1)
        n_used = (pad_end[-1:] // EXPERT_ROWS).astype(I32)

        xs = _dispatch(dest_flat, pad_end.astype(I32), h2.reshape(n_tok, d), n_rows)
        ys = _experts(block_expert, n_used, xs, w_gate_up[l], b_gate_up[l][:, None, :],
                      w_down[l], b_down[l][:, None, :])
        x = _combine(dest_flat, gate_t, x1.reshape(n_tok, d), mod3, row2(final_norm_g), ys, seq).reshape(bsz, seq, d)
    return x
```
